```python
import math
import jax, jax.numpy as jnp
from jax import lax
import numpy as np

D_MODEL = 1024
BATCH = 8
SEQ = 2048
DEPTH = 4
DEC_BATCH = 1
DEC_SEQ = 16384
PAST_LEN = 128

N_MIXERS = 2
N_DIFF_LAYERS = (DEPTH + N_MIXERS - 1) // N_MIXERS
N_MLA_LAYERS = DEPTH // N_MIXERS
DA_HEAD_DIM = 64
DA_HEADS = D_MODEL // DA_HEAD_DIM // 2
DA_SUBLN_EPS = 1e-5
MLA_HEADS = 8
MLA_NOPE = 128
MLA_ROPE = 64
MLA_V = 128
MLA_Q_RANK = 384
MLA_KV_RANK = 256
ROPE_THETA = 10000.0
N_GROUPS = 8
EXPERTS_PER_GROUP = 8
N_EXPERTS = N_GROUPS * EXPERTS_PER_GROUP
TOP_K = 2
D_EXPERT = 512
Q_BLOCK = 128
ROUTE_BLOCK = 128
NORM_EPS = 1e-6

kernel_name = "hybrid_diffattn_mla_hmoe_encoder"


def rms_norm(x, g, eps=NORM_EPS):
    xf = x.astype(jnp.float32)
    y = xf * lax.rsqrt(jnp.mean(xf * xf, axis=-1, keepdims=True) + eps)
    return (y * g.astype(jnp.float32)).astype(x.dtype)


def alibi_slopes(n_heads):
    return jnp.asarray(2.0 ** (-8.0 * np.arange(1, n_heads + 1) / n_heads), dtype=jnp.float32)


def rope_tables(seq_len):
    inv = 1.0 / (ROPE_THETA ** (jnp.arange(0, MLA_ROPE, 2, dtype=jnp.float32) / MLA_ROPE))
    ang = jnp.arange(seq_len, dtype=jnp.float32)[:, None] * inv[None, :]
    return jnp.cos(ang), jnp.sin(ang)


def apply_rope(x, cos, sin):
    x2 = x.astype(jnp.float32).reshape(x.shape[:-1] + (MLA_ROPE // 2, 2))
    x0, x1 = x2[..., 0], x2[..., 1]
    out = jnp.stack([x0 * cos - x1 * sin, x0 * sin + x1 * cos], axis=-1)
    return out.reshape(x.shape).astype(x.dtype)


def diff_attention(x, w_qkv, w_o, lq1, lk1, lq2, lk2, subln_g, lambda_init):
    B, S, _ = x.shape
    H, d = DA_HEADS, DA_HEAD_DIM
    nq = S // Q_BLOCK
    q, k, v = jnp.split(x @ w_qkv, 3, axis=-1)
    q = (q * (d ** -0.5)).reshape(B, nq, Q_BLOCK, 2 * H, d).transpose(1, 0, 3, 2, 4)
    k = k.reshape(B, S, 2 * H, d).transpose(0, 2, 1, 3)
    v = v.reshape(B, S, H, 2 * d).transpose(0, 2, 1, 3)
    lam = (jnp.exp(jnp.sum((lq1 * lk1).astype(jnp.float32)))
           - jnp.exp(jnp.sum((lq2 * lk2).astype(jnp.float32))) + lambda_init)
    slopes = jnp.repeat(alibi_slopes(H), 2)
    k_pos = jnp.arange(S, dtype=jnp.float32)
    starts = jnp.arange(nq, dtype=jnp.float32) * Q_BLOCK

    def block(args):
        qb, start = args
        s = jnp.einsum('bjqd,bjkd->bjqk', qb, k).astype(jnp.float32)
        q_pos = start + jnp.arange(Q_BLOCK, dtype=jnp.float32)
        s = s - slopes[:, None, None] * jnp.abs(q_pos[:, None] - k_pos[None, :])
        p = jax.nn.softmax(s, axis=-1).reshape(B, H, 2, Q_BLOCK, S)
        a = p[:, :, 0] - lam * p[:, :, 1]
        return jnp.einsum('bhqk,bhkd->bhqd', a.astype(v.dtype), v)

    o = lax.map(block, (q, starts))
    o = o.transpose(1, 0, 3, 2, 4).reshape(B, S, H, 2 * d)
    o = rms_norm(o, subln_g, DA_SUBLN_EPS) * (1.0 - lambda_init)
    return o.reshape(B, S, H * 2 * d) @ w_o


def mla_attention(x, w_down, q_norm_g, w_uq, kv_norm_g, w_ukv, w_o, cos, sin):
    B, S, _ = x.shape
    H, DN, DR, DV = MLA_HEADS, MLA_NOPE, MLA_ROPE, MLA_V
    nq = S // Q_BLOCK
    down = x @ w_down
    c_q = rms_norm(down[..., :MLA_Q_RANK], q_norm_g)
    c_kv = rms_norm(down[..., MLA_Q_RANK:MLA_Q_RANK + MLA_KV_RANK], kv_norm_g)
    k_rope = apply_rope(down[..., MLA_Q_RANK + MLA_KV_RANK:], cos, sin)
    q = (c_q @ w_uq).reshape(B, S, H, DN + DR)
    q_nope = q[..., :DN]
    q_rope = apply_rope(q[..., DN:], cos[:, None, :], sin[:, None, :])
    kv = (c_kv @ w_ukv).reshape(B, S, H, DN + DV)
    k_nope, v = kv[..., :DN], kv[..., DN:]
    scale = (DN + DR) ** -0.5
    qn = q_nope.reshape(B, nq, Q_BLOCK, H, DN).transpose(1, 0, 2, 3, 4)
    qr = q_rope.reshape(B, nq, Q_BLOCK, H, DR).transpose(1, 0, 2, 3, 4)

    def block(args):
        qn_b, qr_b = args
        s = (jnp.einsum('bqhd,bkhd->bhqk', qn_b, k_nope)
             + jnp.einsum('bqhd,bkd->bhqk', qr_b, k_rope)).astype(jnp.float32) * scale
        p = jax.nn.softmax(s, axis=-1)
        return jnp.einsum('bhqk,bkhd->bqhd', p.astype(v.dtype), v)

    o = lax.map(block, (qn, qr))
    o = o.transpose(1, 0, 2, 3, 4).reshape(B, S, H * DV)
    return o @ w_o


def routed_experts(xf, expert_id, gate, w1, w3, w2):
    N, D = xf.shape
    A = N * TOP_K
    flat_e = expert_id.reshape(-1).astype(jnp.int32)
    flat_g = gate.reshape(-1)
    order = jnp.argsort(flat_e)
    e_sorted = flat_e[order]
    counts = jnp.bincount(flat_e, length=N_EXPERTS)
    padded = (counts + ROUTE_BLOCK - 1) // ROUTE_BLOCK * ROUTE_BLOCK
    pad_end = jnp.cumsum(padded)
    pad_start = pad_end - padded
    seg_start = jnp.cumsum(counts) - counts
    rank = jnp.arange(A, dtype=jnp.int32) - seg_start[e_sorted]
    dest = pad_start[e_sorted] + rank
    n_blocks = (A + N_EXPERTS * (ROUTE_BLOCK - 1) + ROUTE_BLOCK - 1) // ROUTE_BLOCK
    P = n_blocks * ROUTE_BLOCK
    slot_tok = jnp.zeros((P,), jnp.int32).at[dest].set((order // TOP_K).astype(jnp.int32))
    slot_gate = jnp.zeros((P,), flat_g.dtype).at[dest].set(flat_g[order])
    block_starts = jnp.arange(n_blocks, dtype=pad_end.dtype) * ROUTE_BLOCK
    block_expert = jnp.minimum(jnp.searchsorted(pad_end, block_starts, side='right'), N_EXPERTS - 1)
    xb = xf[slot_tok].reshape(n_blocks, ROUTE_BLOCK, D)

    def block(args):
        xi, e = args
        h = jax.nn.silu(xi @ w1[e]) * (xi @ w3[e])
        return h @ w2[e]

    yb = lax.map(block, (xb, block_expert)).reshape(P, D)
    return jnp.zeros_like(xf).at[slot_tok].add(yb * slot_gate[:, None].astype(yb.dtype))


def hier_moe(x, w_group, w_expert, w1, w3, w2):
    B, S, D = x.shape
    xf = x.reshape(B * S, D)
    N = B * S
    rows = jnp.arange(N)
    pg = jax.nn.softmax((xf @ w_group).astype(jnp.float32), axis=-1)
    g_sel = jnp.argmax(pg, axis=-1).astype(jnp.int32)
    pg_sel = pg[rows, g_sel]
    le = (xf @ w_expert).astype(jnp.float32).reshape(N, N_GROUPS, EXPERTS_PER_GROUP)
    pe = jax.nn.softmax(le[rows, g_sel], axis=-1)
    top_p, top_i = lax.top_k(pe, TOP_K)
    gate = pg_sel[:, None] * top_p / jnp.sum(top_p, axis=-1, keepdims=True)
    expert_id = g_sel[:, None] * EXPERTS_PER_GROUP + top_i.astype(jnp.int32)
    return routed_experts(xf, expert_id, gate, w1, w3, w2).reshape(B, S, D)


def trunk(x, ln_mix, ln_ffn, ln_final, da_wqkv, da_wo, da_lambda_q1, da_lambda_k1, da_lambda_q2,
          da_lambda_k2, da_subln, mla_w_down, mla_q_norm, mla_w_uq, mla_kv_norm, mla_w_ukv, mla_wo,
          moe_w_group, moe_w_expert, moe_w1, moe_w3, moe_w2):
    cos, sin = rope_tables(x.shape[1])
    for i in range(DEPTH):
        j = i // N_MIXERS
        h = rms_norm(x, ln_mix[i])
        if i % N_MIXERS == 0:
            lambda_init = 0.8 - 0.6 * math.exp(-0.3 * i)
            h = diff_attention(h, da_wqkv[j], da_wo[j], da_lambda_q1[j], da_lambda_k1[j],
                               da_lambda_q2[j], da_lambda_k2[j], da_subln[j], lambda_init)
        else:
            h = mla_attention(h, mla_w_down[j], mla_q_norm[j], mla_w_uq[j], mla_kv_norm[j],
                              mla_w_ukv[j], mla_wo[j], cos, sin)
        x = x + h
        x = x + hier_moe(rms_norm(x, ln_ffn[i]), moe_w_group[i], moe_w_expert[i],
                         moe_w1[i], moe_w3[i], moe_w2[i])
    return rms_norm(x, ln_final)


def setup_inputs(seed: int = 0) -> dict:
    key = jax.random.key(seed)
    ks = jax.random.split(key, 24)
    f32 = jnp.float32
    D = D_MODEL
    NA, NB = N_DIFF_LAYERS, N_MLA_LAYERS
    qkr = MLA_Q_RANK + MLA_KV_RANK + MLA_ROPE

    def nrm(k, shape, fan_in):
        return jax.random.normal(k, shape, f32) * (fan_in ** -0.5)

    def gain(k, shape):
        return 1.0 + 0.02 * jax.random.normal(k, shape, f32)

    return {
        "x_prompt": jax.random.normal(ks[0], (BATCH, SEQ, D), f32),
        "x_sample": jax.random.normal(ks[1], (DEC_BATCH, DEC_SEQ, D), f32),
        "ln_mix": gain(ks[2], (DEPTH, D)),
        "ln_ffn": gain(ks[3], (DEPTH, D)),
        "ln_final": gain(ks[4], (D,)),
        "da_wqkv": nrm(ks[5], (NA, D, 3 * D), D),
        "da_wo": nrm(ks[6], (NA, D, D), D),
        "da_lambda_q1": 0.1 * jax.random.normal(ks[7], (NA, DA_HEAD_DIM), f32),
        "da_lambda_k1": 0.1 * jax.random.normal(ks[8], (NA, DA_HEAD_DIM), f32),
        "da_lambda_q2": 0.1 * jax.random.normal(ks[9], (NA, DA_HEAD_DIM), f32),
        "da_lambda_k2": 0.1 * jax.random.normal(ks[10], (NA, DA_HEAD_DIM), f32),
        "da_subln": gain(ks[11], (NA, 2 * DA_HEAD_DIM)),
        "mla_w_down": nrm(ks[12], (NB, D, qkr), D),
        "mla_q_norm": gain(ks[13], (NB, MLA_Q_RANK)),
        "mla_w_uq": nrm(ks[14], (NB, MLA_Q_RANK, MLA_HEADS * (MLA_NOPE + MLA_ROPE)), MLA_Q_RANK),
        "mla_kv_norm": gain(ks[15], (NB, MLA_KV_RANK)),
        "mla_w_ukv": nrm(ks[16], (NB, MLA_KV_RANK, MLA_HEADS * (MLA_NOPE + MLA_V)), MLA_KV_RANK),
        "mla_wo": nrm(ks[17], (NB, MLA_HEADS * MLA_V, D), MLA_HEADS * MLA_V),
        "moe_w_group": nrm(ks[18], (DEPTH, D, N_GROUPS), D),
        "moe_w_expert": nrm(ks[19], (DEPTH, D, N_EXPERTS), D),
        "moe_w1": nrm(ks[20], (DEPTH, N_EXPERTS, D, D_EXPERT), D),
        "moe_w3": nrm(ks[21], (DEPTH, N_EXPERTS, D, D_EXPERT), D),
        "moe_w2": nrm(ks[22], (DEPTH, N_EXPERTS, D_EXPERT, D), D_EXPERT),
    }


def reference(x_prompt, x_sample, ln_mix, ln_ffn, ln_final, da_wqkv, da_wo, da_lambda_q1, da_lambda_k1,
              da_lambda_q2, da_lambda_k2, da_subln, mla_w_down, mla_q_norm, mla_w_uq, mla_kv_norm,
              mla_w_ukv, mla_wo, moe_w_group, moe_w_expert, moe_w1, moe_w3, moe_w2):
    weights = (ln_mix, ln_ffn, ln_final, da_wqkv, da_wo, da_lambda_q1, da_lambda_k1, da_lambda_q2,
               da_lambda_k2, da_subln, mla_w_down, mla_q_norm, mla_w_uq, mla_kv_norm, mla_w_ukv, mla_wo,
               moe_w_group, moe_w_expert, moe_w1, moe_w3, moe_w2)
    y_prompt = trunk(x_prompt, *weights)
    y_sample = trunk(x_sample, *weights)
    return (y_prompt, y_sample)
```

```python
import functools
import math

import numpy as np
import jax
import jax.numpy as jnp
from jax import lax
from jax.experimental import pallas as pl
from jax.experimental.pallas import tpu as pltpu

F32 = jnp.float32
MXU_DTYPE = jnp.bfloat16
LOG2E = 1.4426950408889634

NORM_EPS = 1e-6
DA_SUBLN_EPS = 1e-5
ROPE_THETA = 10000.0
DA_HEAD_DIM = 64
MLA_HEADS = 8
MLA_NOPE = 128
MLA_ROPE = 64
MLA_V = 128
MLA_Q_RANK = 384
MLA_KV_RANK = 256
N_GROUPS = 8
EXPERTS_PER_GROUP = 8
N_EXPERTS = N_GROUPS * EXPERTS_PER_GROUP

LANES = 128
TOKEN_TILE = 512
KV_CHUNK = 512
DA_Q_TILE = 256
MLA_Q_TILE = 512
EXPERT_ROWS = 256
VMEM_LIMIT = 56 * 1024 * 1024


def _rms(x, g, eps):
    return x * lax.rsqrt(jnp.mean(x * x, axis=-1, keepdims=True) + eps) * g


def _nt_dot(a, b):
    return lax.dot_general(a, b, (((1,), (1,)), ((), ())), preferred_element_type=F32)


def _params(n_axes):
    return pltpu.CompilerParams(dimension_semantics=("arbitrary",) * n_axes, vmem_limit_bytes=VMEM_LIMIT)


def _da_qkv_kernel(x_ref, g_ref, wq_ref, wk_ref, wvt_ref, q_ref, k_ref, vt_ref, *, q_scale):
    xn = _rms(x_ref[...], g_ref[...], NORM_EPS).astype(MXU_DTYPE)
    q_ref[...] = (jnp.dot(xn, wq_ref[...], preferred_element_type=F32) * q_scale).astype(q_ref.dtype)
    k_ref[...] = jnp.dot(xn, wk_ref[...], preferred_element_type=F32).astype(k_ref.dtype)
    vt_ref[0] = _nt_dot(wvt_ref[...], xn).astype(vt_ref.dtype)


def _da_qkv(x, g, wq, wk, wvt, q_scale):
    n, d = x.shape
    tm = TOKEN_TILE
    full = lambda a: pl.BlockSpec(a.shape, lambda i: (0,) * a.ndim)
    return pl.pallas_call(
        functools.partial(_da_qkv_kernel, q_scale=q_scale),
        grid=(n // tm,),
        in_specs=[pl.BlockSpec((tm, d), lambda i: (i, 0)), full(g), full(wq), full(wk), full(wvt)],
        out_specs=[pl.BlockSpec((tm, d), lambda i: (i, 0)), pl.BlockSpec((tm, d), lambda i: (i, 0)),
                   pl.BlockSpec((1, d, tm), lambda i: (i, 0, 0))],
        out_shape=[jax.ShapeDtypeStruct((n, d), MXU_DTYPE), jax.ShapeDtypeStruct((n, d), MXU_DTYPE),
                   jax.ShapeDtypeStruct((n // tm, d, tm), MXU_DTYPE)],
        compiler_params=_params(1), name="da_qkv",
    )(x, g, wq, wk, wvt)


def _mla_proj_kernel(x_ref, g_ref, wd_ref, qg_ref, kvg_ref, wuq_ref, wkn_ref, wvt_ref, cos_ref, sin_ref,
                     q_ref, kn_ref, kr_ref, vt_ref, *, q_scale):
    xn = _rms(x_ref[...], g_ref[...], NORM_EPS).astype(MXU_DTYPE)
    down = jnp.dot(xn, wd_ref[...], preferred_element_type=F32)
    qr, kvr = MLA_Q_RANK, MLA_KV_RANK
    c_q = _rms(down[:, :qr], qg_ref[...], NORM_EPS).astype(MXU_DTYPE)
    c_kv = _rms(down[:, qr:qr + kvr], kvg_ref[...], NORM_EPS).astype(MXU_DTYPE)
    cos, sin = cos_ref[...], sin_ref[...]
    kr = down[:, qr + kvr:qr + kvr + LANES] * cos + down[:, qr + kvr + LANES:] * sin
    kr_ref[...] = kr.astype(kr_ref.dtype)
    hd = MLA_HEADS * LANES
    q = jnp.dot(c_q, wuq_ref[...], preferred_element_type=F32)
    q_ref[:, :hd] = (q[:, :hd] * q_scale).astype(q_ref.dtype)
    for h in range(MLA_HEADS):
        a = q[:, hd + h * LANES:hd + (h + 1) * LANES]
        b = q[:, 2 * hd + h * LANES:2 * hd + (h + 1) * LANES]
        q_ref[:, hd + h * LANES:hd + (h + 1) * LANES] = ((a * cos + b * sin) * q_scale).astype(q_ref.dtype)
    kn_ref[...] = jnp.dot(c_kv, wkn_ref[...], preferred_element_type=F32).astype(kn_ref.dtype)
    vt_ref[0] = _nt_dot(wvt_ref[...], c_kv).astype(vt_ref.dtype)


def _mla_proj(x, g, wd, qg, kvg, wuq, wkn, wvt, cos, sin, q_scale):
    n, d = x.shape
    tm = TOKEN_TILE
    hd = MLA_HEADS * LANES
    full = lambda a: pl.BlockSpec(a.shape, lambda i: (0,) * a.ndim)
    row = lambda w: pl.BlockSpec((tm, w), lambda i: (i, 0))
    return pl.pallas_call(
        functools.partial(_mla_proj_kernel, q_scale=q_scale),
        grid=(n // tm,),
        in_specs=[row(d), full(g), full(wd), full(qg), full(kvg), full(wuq), full(wkn), full(wvt),
                  row(LANES), row(LANES)],
        out_specs=[row(2 * hd), row(hd), row(LANES), pl.BlockSpec((1, hd, tm), lambda i: (i, 0, 0))],
        out_shape=[jax.ShapeDtypeStruct((n, 2 * hd), MXU_DTYPE), jax.ShapeDtypeStruct((n, hd), MXU_DTYPE),
                   jax.ShapeDtypeStruct((n, LANES), MXU_DTYPE),
                   jax.ShapeDtypeStruct((n // tm, hd, tm), MXU_DTYPE)],
        compiler_params=_params(1), name="mla_proj",
    )(x, g, wd, qg, kvg, wuq, wkn, wvt, cos, sin)


def _out_proj_kernel(x_ref, oa_ref, ob_ref, w_ref, y_ref, *, a_tiles):
    @pl.when(pl.program_id(0) < a_tiles)
    def _():
        y_ref[...] = x_ref[...] + jnp.dot(oa_ref[...], w_ref[...], preferred_element_type=F32)

    @pl.when(pl.program_id(0) >= a_tiles)
    def _():
        y_ref[...] = x_ref[...] + jnp.dot(ob_ref[...], w_ref[...], preferred_element_type=F32)


def _out_proj(x, oa, ob, w):
    n, d = x.shape
    tm = TOKEN_TILE
    a_tiles = oa.shape[0] // tm
    assert oa.shape[0] % tm == 0 and ob.shape[0] % tm == 0 and oa.shape[0] + ob.shape[0] == n
    return pl.pallas_call(
        functools.partial(_out_proj_kernel, a_tiles=a_tiles),
        grid=(n // tm,),
        in_specs=[pl.BlockSpec((tm, d), lambda i: (i, 0)),
                  pl.BlockSpec((tm, oa.shape[1]), lambda i: (jnp.minimum(i, a_tiles - 1), 0)),
                  pl.BlockSpec((tm, ob.shape[1]), lambda i: (jnp.maximum(i - a_tiles, 0), 0)),
                  pl.BlockSpec(w.shape, lambda i: (0, 0))],
        out_specs=pl.BlockSpec((tm, d), lambda i: (i, 0)),
        out_shape=jax.ShapeDtypeStruct((n, d), F32),
        compiler_params=_params(1), name="out_proj",
    )(x, oa, ob, w)


def _flash_t(q, k_ref, vt_ref, n_chunks, tk, m_ref, l_ref, acc_ref, bias_fn):
    m_ref[...] = jnp.full(m_ref.shape, -jnp.inf, F32)
    l_ref[...] = jnp.zeros(l_ref.shape, F32)
    acc_ref[...] = jnp.zeros(acc_ref.shape, F32)

    def body(c, carry):
        k = k_ref[pl.ds(pl.multiple_of(c * tk, tk), tk), :]
        st = _nt_dot(k, q)
        if bias_fn is not None:
            st = st - bias_fn(c)
        m_prev = m_ref[...]
        m_new = jnp.maximum(m_prev, jnp.max(st, axis=0, keepdims=True))
        p = jnp.exp2(st - m_new)
        alpha = jnp.exp2(m_prev - m_new)
        l_ref[...] = alpha * l_ref[...] + jnp.sum(p, axis=0, keepdims=True)
        acc_ref[...] = acc_ref[...] * alpha + jnp.dot(vt_ref[c], p.astype(MXU_DTYPE), preferred_element_type=F32)
        m_ref[...] = m_new
        return carry

    lax.fori_loop(0, n_chunks, body, 0)


def _da_attn_kernel(sc_ref, q_ref, k_ref, vt_ref, g_ref, o_ref, m_ref, l_ref, acc_ref, dist_ref, *,
                    tq, tk, n_chunks, n_heads, out_scale):
    h, qi = pl.program_id(1), pl.program_id(2)
    slope, lam = sc_ref[h], sc_ref[n_heads]
    q = q_ref[...]
    lane = lax.broadcasted_iota(jnp.int32, q.shape, 1)
    zero = jnp.zeros_like(q)
    d = q.shape[1] // 2
    qs = jnp.concatenate([jnp.where(lane < d, q, zero), jnp.where(lane >= d, q, zero)], axis=0)
    row = lax.broadcasted_iota(jnp.int32, (tk, 2 * tq), 0)
    col = lax.broadcasted_iota(jnp.int32, (tk, 2 * tq), 1)
    col = jnp.where(col >= tq, col - tq, col)
    dist_ref[...] = (row - col - qi * tq).astype(F32)

    def bias_fn(c):
        return slope * jnp.abs(dist_ref[...] + lax.convert_element_type(c * tk, F32))

    _flash_t(qs, k_ref, vt_ref, n_chunks, tk, m_ref, l_ref, acc_ref, bias_fn)
    o = acc_ref[...] * (1.0 / l_ref[...])
    o = o[:, :tq] - lam * o[:, tq:]
    o = o * lax.rsqrt(jnp.mean(o * o, axis=0, keepdims=True) + DA_SUBLN_EPS) * g_ref[...] * out_scale
    o_ref[...] = o.T.astype(o_ref.dtype)


def _da_attn(scalars, q, k, vt, subln_col, batch, seq, row0, out_scale):
    dm = q.shape[1]
    tq, tk = DA_Q_TILE, KV_CHUNK
    hw = 2 * DA_HEAD_DIM
    n_heads = dm // hw
    assert row0 % seq == 0 and seq % tq == 0 and seq % tk == 0
    s0, q0, nq, n_chunks = row0 // seq, row0 // tq, seq // tq, seq // tk
    return pl.pallas_call(
        functools.partial(_da_attn_kernel, tq=tq, tk=tk, n_chunks=n_chunks, n_heads=n_heads, out_scale=out_scale),
        grid_spec=pltpu.PrefetchScalarGridSpec(
            num_scalar_prefetch=1, grid=(batch, n_heads, nq),
            in_specs=[pl.BlockSpec((tq, hw), lambda b, h, i, sc: (q0 + b * nq + i, h)),
                      pl.BlockSpec((seq, hw), lambda b, h, i, sc: (s0 + b, h)),
                      pl.BlockSpec((n_chunks, hw, tk), lambda b, h, i, sc: (s0 + b, h, 0)),
                      pl.BlockSpec(subln_col.shape, lambda b, h, i, sc: (0, 0))],
            out_specs=pl.BlockSpec((tq, hw), lambda b, h, i, sc: (b * nq + i, h)),
            scratch_shapes=[pltpu.VMEM((1, 2 * tq), F32), pltpu.VMEM((1, 2 * tq), F32),
                            pltpu.VMEM((hw, 2 * tq), F32), pltpu.VMEM((tk, 2 * tq), F32)]),
        out_shape=jax.ShapeDtypeStruct((batch * seq, dm), MXU_DTYPE),
        compiler_params=_params(3), name="da_attn",
    )(scalars, q, k, vt, subln_col)


def _mla_attn_kernel(qn_ref, qr_ref, kn_ref, kr_ref, vt_ref, o_ref, kfull_ref, m_ref, l_ref, acc_ref, *,
                     tk, n_chunks):

    @pl.when(pl.program_id(2) == 0)
    def _():
        def copy(c, carry):
            rows = pl.ds(pl.multiple_of(c * tk, tk), tk)
            kfull_ref[rows, :LANES] = kn_ref[rows, :]
            kfull_ref[rows, LANES:] = kr_ref[rows, :]
            return carry
        lax.fori_loop(0, n_chunks, copy, 0)

    q = jnp.concatenate([qn_ref[...], qr_ref[...]], axis=1)
    _flash_t(q, kfull_ref, vt_ref, n_chunks, tk, m_ref, l_ref, acc_ref, None)
    o = acc_ref[...] * (1.0 / l_ref[...])
    o_ref[...] = o.T.astype(o_ref.dtype)


def _mla_attn(q, kn, kr, vt, batch, seq, row0):
    n_heads = MLA_HEADS
    tq, tk = MLA_Q_TILE, KV_CHUNK
    assert row0 % seq == 0 and seq % tq == 0 and seq % tk == 0
    s0, q0, nq, n_chunks = row0 // seq, row0 // tq, seq // tq, seq // tk
    return pl.pallas_call(
        functools.partial(_mla_attn_kernel, tk=tk, n_chunks=n_chunks),
        grid=(batch, n_heads, nq),
        in_specs=[pl.BlockSpec((tq, LANES), lambda b, h, i: (q0 + b * nq + i, h)),
                  pl.BlockSpec((tq, LANES), lambda b, h, i: (q0 + b * nq + i, n_heads + h)),
                  pl.BlockSpec((seq, LANES), lambda b, h, i: (s0 + b, h)),
                  pl.BlockSpec((seq, LANES), lambda b, h, i: (s0 + b, 0)),
                  pl.BlockSpec((n_chunks, MLA_V, tk), lambda b, h, i: (s0 + b, h, 0))],
        out_specs=pl.BlockSpec((tq, MLA_V), lambda b, h, i: (b * nq + i, h)),
        out_shape=jax.ShapeDtypeStruct((batch * seq, n_heads * MLA_V), MXU_DTYPE),
        scratch_shapes=[pltpu.VMEM((seq, 2 * LANES), MXU_DTYPE), pltpu.VMEM((1, tq), F32),
                        pltpu.VMEM((1, tq), F32), pltpu.VMEM((MLA_V, tq), F32)],
        compiler_params=_params(3), name="mla_attn",
    )(q, q, kn, kr, vt)


R_E0, R_E1, R_RANK0, R_RANK1, R_GATE0, R_GATE1 = range(6)


def _router_kernel(x_ref, g_ref, w_ref, tri_ref, rec_ref, cnt_ref, base_ref):
    i = pl.program_id(0)

    @pl.when(i == 0)
    def _():
        base_ref[...] = jnp.zeros(base_ref.shape, F32)

    xn = _rms(x_ref[...], g_ref[...], NORM_EPS)
    logits = jnp.dot(xn, w_ref[...], preferred_element_type=F32, precision=lax.Precision.HIGHEST)
    tm = logits.shape[0]
    lane = lax.broadcasted_iota(jnp.int32, logits.shape, 1)
    neg = jnp.float32(-jnp.inf)

    def first_lane(mask):
        return jnp.min(jnp.where(mask, lane, LANES), axis=1, keepdims=True)

    is_g = lane < N_GROUPS
    lg = jnp.where(is_g, logits, neg)
    mg = jnp.max(lg, axis=1, keepdims=True)
    pg_sel = 1.0 / jnp.sum(jnp.exp(lg - mg), axis=1, keepdims=True)
    g_sel = first_lane(lg == mg)
    e_lo = N_GROUPS + EXPERTS_PER_GROUP * g_sel
    in_e = (lane >= e_lo) & (lane < e_lo + EXPERTS_PER_GROUP)
    le = jnp.where(in_e, logits, neg)
    me = jnp.max(le, axis=1, keepdims=True)
    ee = jnp.exp(le - me)
    pe = ee / jnp.sum(ee, axis=1, keepdims=True)
    pe = jnp.where(in_e, pe, -1.0)
    p0 = jnp.max(pe, axis=1, keepdims=True)
    i0 = first_lane(pe == p0)
    pe1 = jnp.where(lane == i0, -1.0, pe)
    p1 = jnp.max(pe1, axis=1, keepdims=True)
    i1 = first_lane(pe1 == p1)
    gate0 = pg_sel * p0 / (p0 + p1)
    gate1 = pg_sel * p1 / (p0 + p1)
    hit0, hit1 = lane == i0, lane == i1
    oh = jnp.concatenate([jnp.where(hit0, 1.0, 0.0), jnp.where(hit1, 1.0, 0.0)], axis=1)
    before = jnp.dot(tri_ref[...], oh.astype(MXU_DTYPE), preferred_element_type=F32)
    base = base_ref[...]
    tot0 = jnp.sum(oh[:, :LANES], axis=0, keepdims=True)
    tot1 = jnp.sum(oh[:, LANES:], axis=0, keepdims=True)
    rank0 = jnp.sum(jnp.where(hit0, before[:, :LANES] + base, 0.0), axis=1, keepdims=True)
    rank1 = jnp.sum(jnp.where(hit1, before[:, LANES:] + (base + tot0), 0.0), axis=1, keepdims=True)
    base = base + tot0 + tot1
    base_ref[...] = base
    cnt_ref[...] = base
    rec = jnp.zeros(logits.shape, F32)
    for pos, val in ((R_E0, (i0 - N_GROUPS).astype(F32)), (R_E1, (i1 - N_GROUPS).astype(F32)),
                     (R_RANK0, rank0), (R_RANK1, rank1), (R_GATE0, gate0), (R_GATE1, gate1)):
        rec = jnp.where(lane == pos, val, rec)
    rec_ref[...] = rec


def _router(x, g, w_router, tri):
    n, d = x.shape
    tm = TOKEN_TILE
    full = lambda a: pl.BlockSpec(a.shape, lambda i: (0,) * a.ndim)
    return pl.pallas_call(
        _router_kernel,
        grid=(n // tm,),
        in_specs=[pl.BlockSpec((tm, d), lambda i: (i, 0)), full(g), full(w_router), full(tri)],
        out_specs=[pl.BlockSpec((tm, LANES), lambda i: (i, 0)), pl.BlockSpec((1, LANES), lambda i: (0, 0))],
        out_shape=[jax.ShapeDtypeStruct((n, LANES), F32), jax.ShapeDtypeStruct((1, LANES), F32)],
        scratch_shapes=[pltpu.VMEM((1, LANES), F32)],
        compiler_params=_params(1), name="moe_router",
    )(x, g, w_router, tri)


def _row_copy(src_ref, src_row, dst_ref, dst_row, sem):
    return pltpu.make_async_copy(src_ref.at[pl.ds(src_row, 1), :], dst_ref.at[pl.ds(dst_row, 1), :], sem)


def _dispatch_kernel(dest_ref, x_ref, g_ref, xs_in_ref, xs_ref, xn_ref, sem, *, tm):
    del xs_in_ref
    xn_ref[...] = _rms(x_ref[...], g_ref[...], NORM_EPS)

    def issue(r, carry):
        for c in range(2):
            _row_copy(xn_ref, r, xs_ref, dest_ref[2 * r + c], sem).start()
        return carry
    lax.fori_loop(0, tm, issue, 0)

    def drain(r, carry):
        for c in range(2):
            _row_copy(xn_ref, 0, xs_ref, 0, sem).wait()
        return carry
    lax.fori_loop(0, tm, drain, 0)


def _dispatch(dest_flat, x, g, xs_zero):
    n, d = x.shape
    tm = TOKEN_TILE
    return pl.pallas_call(
        functools.partial(_dispatch_kernel, tm=tm),
        grid=(n // tm,),
        in_specs=[pl.BlockSpec((2 * tm,), lambda i: (i,), memory_space=pltpu.SMEM),
                  pl.BlockSpec((tm, d), lambda i: (i, 0)), pl.BlockSpec(g.shape, lambda i: (0, 0)),
                  pl.BlockSpec(memory_space=pl.ANY)],
        out_specs=pl.BlockSpec(memory_space=pl.ANY),
        out_shape=jax.ShapeDtypeStruct(xs_zero.shape, F32),
        scratch_shapes=[pltpu.VMEM((tm, d), F32), pltpu.SemaphoreType.DMA(())],
        input_output_aliases={3: 0},
        compiler_params=_params(1), name="moe_dispatch",
    )(dest_flat, x, g, xs_zero)


def _expert_kernel(be_ref, nv_ref, xs_ref, w1_ref, w3_ref, w2_ref, y_ref):
    del be_ref

    @pl.when(pl.program_id(0) < nv_ref[0])
    def _():
        xb = xs_ref[...].astype(MXU_DTYPE)
        h1 = jnp.dot(xb, w1_ref[0].astype(MXU_DTYPE), preferred_element_type=F32)
        h3 = jnp.dot(xb, w3_ref[0].astype(MXU_DTYPE), preferred_element_type=F32)
        h = (h1 * jax.nn.sigmoid(h1) * h3).astype(MXU_DTYPE)
        y_ref[...] = jnp.dot(h, w2_ref[0].astype(MXU_DTYPE), preferred_element_type=F32)

    @pl.when(pl.program_id(0) >= nv_ref[0])
    def _():
        y_ref[...] = jnp.zeros(y_ref.shape, F32)


def _experts(block_expert, n_valid, xs, w1, w3, w2):
    p, d = xs.shape
    rb = EXPERT_ROWS
    de = w1.shape[2]
    return pl.pallas_call(
        _expert_kernel,
        grid_spec=pltpu.PrefetchScalarGridSpec(
            num_scalar_prefetch=2, grid=(p // rb,),
            in_specs=[pl.BlockSpec((rb, d), lambda b, be, nv: (b, 0)),
                      pl.BlockSpec((1, d, de), lambda b, be, nv: (be[b], 0, 0)),
                      pl.BlockSpec((1, d, de), lambda b, be, nv: (be[b], 0, 0)),
                      pl.BlockSpec((1, de, d), lambda b, be, nv: (be[b], 0, 0))],
            out_specs=pl.BlockSpec((rb, d), lambda b, be, nv: (b, 0))),
        out_shape=jax.ShapeDtypeStruct((p, d), F32),
        compiler_params=_params(1), name="moe_experts",
    )(block_expert, n_valid, xs, w1, w3, w2)


def _combine_kernel(dest_ref, x_ref, rec_ref, gfin_ref, yb_ref, o_ref, ybuf_ref, sem, *, tm, final):
    def issue(r, carry):
        for c in range(2):
            _row_copy(yb_ref, dest_ref[2 * r + c], ybuf_ref.at[c], r, sem).start()
        return carry
    lax.fori_loop(0, tm, issue, 0)

    def drain(r, carry):
        for c in range(2):
            _row_copy(yb_ref, 0, ybuf_ref.at[c], 0, sem).wait()
        return carry
    lax.fori_loop(0, tm, drain, 0)

    rec = rec_ref[...]
    out = (x_ref[...] + rec[:, R_GATE0:R_GATE0 + 1] * ybuf_ref[0] + rec[:, R_GATE1:R_GATE1 + 1] * ybuf_ref[1])
    if final:
        out = _rms(out, gfin_ref[...], NORM_EPS)
    o_ref[...] = out


def _combine(dest_flat, x, rec, g_final, yb, final):
    n, d = x.shape
    tm = TOKEN_TILE
    return pl.pallas_call(
        functools.partial(_combine_kernel, tm=tm, final=final),
        grid=(n // tm,),
        in_specs=[pl.BlockSpec((2 * tm,), lambda i: (i,), memory_space=pltpu.SMEM),
                  pl.BlockSpec((tm, d), lambda i: (i, 0)), pl.BlockSpec((tm, LANES), lambda i: (i, 0)),
                  pl.BlockSpec(g_final.shape, lambda i: (0, 0)), pl.BlockSpec(memory_space=pl.ANY)],
        out_specs=pl.BlockSpec((tm, d), lambda i: (i, 0)),
        out_shape=jax.ShapeDtypeStruct((n, d), F32),
        scratch_shapes=[pltpu.VMEM((2, tm, d), F32), pltpu.SemaphoreType.DMA(())],
        compiler_params=_params(1), name="moe_combine",
    )(dest_flat, x, rec, g_final, yb)


def _moe(x, g, w_router, tri, w1, w3, w2, g_final, final):
    n, d = x.shape
    rb = EXPERT_ROWS
    rec, counts = _router(x, g, w_router, tri)
    counts = counts[0, N_GROUPS:N_GROUPS + N_EXPERTS].astype(jnp.int32)
    padded = (counts + rb - 1) // rb * rb
    pad_end = jnp.cumsum(padded)
    pad_start = pad_end - padded
    n_blocks = (2 * n + N_EXPERTS * (rb - 1)) // rb
    expert = rec[:, R_E0:R_E1 + 1].astype(jnp.int32)
    rank = rec[:, R_RANK0:R_RANK1 + 1].astype(jnp.int32)
    dest = (pad_start[expert] + rank).reshape(-1)
    block_start = jnp.arange(n_blocks, dtype=jnp.int32) * rb
    block_expert = jnp.minimum(jnp.searchsorted(pad_end, block_start, side="right"), N_EXPERTS - 1).astype(jnp.int32)
    n_valid = (pad_end[-1:] // rb).astype(jnp.int32)
    xs = _dispatch(dest, x, g, jnp.zeros((n_blocks * rb, d), F32))
    yb = _experts(block_expert, n_valid, xs, w1, w3, w2)
    return _combine(dest, x, rec, g_final, yb, final)


def _rope_tables(positions):
    inv = 1.0 / (ROPE_THETA ** (jnp.arange(0, MLA_ROPE, 2, dtype=F32) / MLA_ROPE))
    ang = positions.astype(F32)[:, None] * inv[None, :]
    cos = jnp.repeat(jnp.cos(ang), 2, axis=1)
    sin = jnp.stack([-jnp.sin(ang), jnp.sin(ang)], axis=-1).reshape(ang.shape[0], MLA_ROPE)
    pad = ((0, 0), (0, LANES - MLA_ROPE))
    return jnp.pad(cos, pad), jnp.pad(sin, pad)


def _pair_swap(w):
    return w.reshape(w.shape[0], -1, 2)[:, :, ::-1].reshape(w.shape)


def kernel(x_prompt, x_sample, ln_mix, ln_ffn, ln_final, da_wqkv, da_wo, da_lambda_q1, da_lambda_k1, da_lambda_q2, da_lambda_k2, da_subln, mla_w_down, mla_q_norm, mla_w_uq, mla_kv_norm, mla_w_ukv, mla_wo, moe_w_group, moe_w_expert, moe_w1, moe_w3, moe_w2):
    bp, sp, d = x_prompt.shape
    bs, ss, _ = x_sample.shape
    n_p, n_s = bp * sp, bs * ss
    groups = ((bp, sp, 0), (bs, ss, n_p))
    x = jnp.concatenate([x_prompt.reshape(n_p, d), x_sample.reshape(n_s, d)], axis=0)
    depth = ln_mix.shape[0]
    cdt = MXU_DTYPE

    positions = jnp.concatenate([jnp.tile(jnp.arange(sp), bp), jnp.tile(jnp.arange(ss), bs)])
    cos, sin = _rope_tables(positions)
    tri = jnp.tril(jnp.ones((TOKEN_TILE, TOKEN_TILE), F32), -1).astype(cdt)
    n_da_heads = d // (2 * DA_HEAD_DIM)
    slopes = 2.0 ** (-8.0 * np.arange(1, n_da_heads + 1) / n_da_heads) * LOG2E
    hd = MLA_HEADS * LANES
    zpad = LANES - MLA_ROPE

    for i in range(depth):
        j = i // 2
        g_mix = ln_mix[i][None, :]
        if i % 2 == 0:
            lambda_init = 0.8 - 0.6 * math.exp(-0.3 * i)
            lam = (jnp.exp(jnp.sum(da_lambda_q1[j] * da_lambda_k1[j])) - jnp.exp(jnp.sum(da_lambda_q2[j] * da_lambda_k2[j]))
                   + lambda_init)
            scalars = jnp.concatenate([jnp.asarray(slopes, F32), lam[None].astype(F32)])
            w = da_wqkv[j]
            q, k, vt = _da_qkv(x, g_mix, w[:, :d].astype(cdt), w[:, d:2 * d].astype(cdt), w[:, 2 * d:].T.astype(cdt),
                               DA_HEAD_DIM ** -0.5 * LOG2E)
            oa, ob = [_da_attn(scalars, q, k, vt, da_subln[j][:, None], batch, seq, row0, 1.0 - lambda_init)
                      for batch, seq, row0 in groups]
            x = _out_proj(x, oa, ob, da_wo[j].astype(cdt))
        else:
            qr, kvr = MLA_Q_RANK, MLA_KV_RANK
            wd = mla_w_down[j]
            w_rope = wd[:, qr + kvr:]
            wd = jnp.concatenate([wd[:, :qr + kvr], jnp.pad(w_rope, ((0, 0), (0, zpad))),
                                  jnp.pad(_pair_swap(w_rope), ((0, 0), (0, zpad)))], axis=1)
            wuq = mla_w_uq[j].reshape(qr, MLA_HEADS, MLA_NOPE + MLA_ROPE)
            wq_rope = wuq[:, :, MLA_NOPE:]
            pad3 = ((0, 0), (0, 0), (0, zpad))
            wuq = jnp.concatenate([wuq[:, :, :MLA_NOPE].reshape(qr, hd), jnp.pad(wq_rope, pad3).reshape(qr, hd),
                                   jnp.pad(_pair_swap(wq_rope.reshape(qr, -1)).reshape(wq_rope.shape), pad3).reshape(qr, hd)],
                                  axis=1)
            wukv = mla_w_ukv[j].reshape(kvr, MLA_HEADS, MLA_NOPE + MLA_V)
            wkn = wukv[:, :, :MLA_NOPE].reshape(kvr, hd)
            wvt = wukv[:, :, MLA_NOPE:].reshape(kvr, MLA_HEADS * MLA_V).T
            q, kn, kr, vt = _mla_proj(x, g_mix, wd.astype(cdt), mla_q_norm[j][None, :], mla_kv_norm[j][None, :],
                                      wuq.astype(cdt), wkn.astype(cdt), wvt.astype(cdt), cos, sin,
                                      (MLA_NOPE + MLA_ROPE) ** -0.5 * LOG2E)
            oa, ob = [_mla_attn(q, kn, kr, vt, batch, seq, row0) for batch, seq, row0 in groups]
            x = _out_proj(x, oa, ob, mla_wo[j].astype(cdt))
        w_router = jnp.pad(jnp.concatenate([moe_w_group[i], moe_w_expert[i]], axis=1),
                           ((0, 0), (0, LANES - N_GROUPS - N_EXPERTS)))
        x = _moe(x, ln_ffn[i][None, :], w_router, tri, moe_w1[i], moe_w3[i], moe_w2[i], ln_final[None, :],
                 final=(i == depth - 1))
    return x[:n_p].reshape(bp, sp, d), x[n_p:].reshape(bs, ss, d)
```

```python
import functools
import math

import numpy as np
import jax
import jax.numpy as jnp
from jax import lax
from jax.experimental import pallas as pl
from jax.experimental.pallas import tpu as pltpu

F32 = jnp.float32
MXU_DTYPE = jnp.bfloat16
LOG2E = 1.4426950408889634

NORM_EPS = 1e-6
DA_SUBLN_EPS = 1e-5
ROPE_THETA = 10000.0
DA_HEAD_DIM = 64
MLA_HEADS = 8
MLA_NOPE = 128
MLA_ROPE = 64
MLA_V = 128
MLA_Q_RANK = 384
MLA_KV_RANK = 256
N_GROUPS = 8
EXPERTS_PER_GROUP = 8
N_EXPERTS = N_GROUPS * EXPERTS_PER_GROUP

LANES = 128
TOKEN_TILE = 512
KV_CHUNK = 512
DA_Q_TILE = 256
MLA_Q_TILE = 512
EXPERT_ROWS = 256
VMEM_LIMIT = 56 * 1024 * 1024


def _rms(x, g, eps):
    return x * lax.rsqrt(jnp.mean(x * x, axis=-1, keepdims=True) + eps) * g


def _nt_dot(a, b):
    return lax.dot_general(a, b, (((1,), (1,)), ((), ())), preferred_element_type=F32)


def _params(n_axes):
    return pltpu.CompilerParams(dimension_semantics=("arbitrary",) * n_axes, vmem_limit_bytes=VMEM_LIMIT)


def _da_qkv_kernel(x_ref, g_ref, wq_ref, wk_ref, wvt_ref, q_ref, k_ref, vt_ref, *, q_scale):
    xn = _rms(x_ref[...], g_ref[...], NORM_EPS).astype(MXU_DTYPE)
    q_ref[...] = (jnp.dot(xn, wq_ref[...], preferred_element_type=F32) * q_scale).astype(q_ref.dtype)
    k_ref[...] = jnp.dot(xn, wk_ref[...], preferred_element_type=F32).astype(k_ref.dtype)
    vt_ref[0] = _nt_dot(wvt_ref[...], xn).astype(vt_ref.dtype)


def _da_qkv(x, g, wq, wk, wvt, q_scale):
    n, d = x.shape
    tm = TOKEN_TILE
    full = lambda a: pl.BlockSpec(a.shape, lambda i: (0,) * a.ndim)
    return pl.pallas_call(
        functools.partial(_da_qkv_kernel, q_scale=q_scale),
        grid=(n // tm,),
        in_specs=[pl.BlockSpec((tm, d), lambda i: (i, 0)), full(g), full(wq), full(wk), full(wvt)],
        out_specs=[pl.BlockSpec((tm, d), lambda i: (i, 0)), pl.BlockSpec((tm, d), lambda i: (i, 0)),
                   pl.BlockSpec((1, d, tm), lambda i: (i, 0, 0))],
        out_shape=[jax.ShapeDtypeStruct((n, d), MXU_DTYPE), jax.ShapeDtypeStruct((n, d), MXU_DTYPE),
                   jax.ShapeDtypeStruct((n // tm, d, tm), MXU_DTYPE)],
        compiler_params=_params(1), name="da_qkv",
    )(x, g, wq, wk, wvt)


def _mla_proj_kernel(x_ref, g_ref, wd_ref, qg_ref, kvg_ref, wuq_ref, wkn_ref, wvt_ref, cos_ref, sin_ref,
                     q_ref, kn_ref, kr_ref, vt_ref, *, q_scale):
    xn = _rms(x_ref[...], g_ref[...], NORM_EPS).astype(MXU_DTYPE)
    down = jnp.dot(xn, wd_ref[...], preferred_element_type=F32)
    qr, kvr = MLA_Q_RANK, MLA_KV_RANK
    c_q = _rms(down[:, :qr], qg_ref[...], NORM_EPS).astype(MXU_DTYPE)
    c_kv = _rms(down[:, qr:qr + kvr], kvg_ref[...], NORM_EPS).astype(MXU_DTYPE)
    cos, sin = cos_ref[...], sin_ref[...]
    kr = down[:, qr + kvr:qr + kvr + LANES] * cos + down[:, qr + kvr + LANES:] * sin
    kr_ref[...] = kr.astype(kr_ref.dtype)
    hd = MLA_HEADS * LANES
    q = jnp.dot(c_q, wuq_ref[...], preferred_element_type=F32)
    q_ref[:, :hd] = (q[:, :hd] * q_scale).astype(q_ref.dtype)
    for h in range(MLA_HEADS):
        a = q[:, hd + h * LANES:hd + (h + 1) * LANES]
        b = q[:, 2 * hd + h * LANES:2 * hd + (h + 1) * LANES]
        q_ref[:, hd + h * LANES:hd + (h + 1) * LANES] = ((a * cos + b * sin) * q_scale).astype(q_ref.dtype)
    kn_ref[...] = jnp.dot(c_kv, wkn_ref[...], preferred_element_type=F32).astype(kn_ref.dtype)
    vt_ref[0] = _nt_dot(wvt_ref[...], c_kv).astype(vt_ref.dtype)


def _mla_proj(x, g, wd, qg, kvg, wuq, wkn, wvt, cos, sin, q_scale):
    n, d = x.shape
    tm = TOKEN_TILE
    hd = MLA_HEADS * LANES
    full = lambda a: pl.BlockSpec(a.shape, lambda i: (0,) * a.ndim)
    row = lambda w: pl.BlockSpec((tm, w), lambda i: (i, 0))
    return pl.pallas_call(
        functools.partial(_mla_proj_kernel, q_scale=q_scale),
        grid=(n // tm,),
        in_specs=[row(d), full(g), full(wd), full(qg), full(kvg), full(wuq), full(wkn), full(wvt),
                  row(LANES), row(LANES)],
        out_specs=[row(2 * hd), row(hd), row(LANES), pl.BlockSpec((1, hd, tm), lambda i: (i, 0, 0))],
        out_shape=[jax.ShapeDtypeStruct((n, 2 * hd), MXU_DTYPE), jax.ShapeDtypeStruct((n, hd), MXU_DTYPE),
                   jax.ShapeDtypeStruct((n, LANES), MXU_DTYPE),
                   jax.ShapeDtypeStruct((n // tm, hd, tm), MXU_DTYPE)],
        compiler_params=_params(1), name="mla_proj",
    )(x, g, wd, qg, kvg, wuq, wkn, wvt, cos, sin)


def _out_proj_kernel(x_ref, oa_ref, ob_ref, w_ref, y_ref, *, a_tiles):
    @pl.when(pl.program_id(0) < a_tiles)
    def _():
        y_ref[...] = x_ref[...] + jnp.dot(oa_ref[...], w_ref[...], preferred_element_type=F32)

    @pl.when(pl.program_id(0) >= a_tiles)
    def _():
        y_ref[...] = x_ref[...] + jnp.dot(ob_ref[...], w_ref[...], preferred_element_type=F32)


def _out_proj(x, oa, ob, w):
    n, d = x.shape
    tm = TOKEN_TILE
    a_tiles = oa.shape[0] // tm
    assert oa.shape[0] % tm == 0 and ob.shape[0] % tm == 0 and oa.shape[0] + ob.shape[0] == n
    return pl.pallas_call(
        functools.partial(_out_proj_kernel, a_tiles=a_tiles),
        grid=(n // tm,),
        in_specs=[pl.BlockSpec((tm, d), lambda i: (i, 0)),
                  pl.BlockSpec((tm, oa.shape[1]), lambda i: (jnp.minimum(i, a_tiles - 1), 0)),
                  pl.BlockSpec((tm, ob.shape[1]), lambda i: (jnp.maximum(i - a_tiles, 0), 0)),
                  pl.BlockSpec(w.shape, lambda i: (0, 0))],
        out_specs=pl.BlockSpec((tm, d), lambda i: (i, 0)),
        out_shape=jax.ShapeDtypeStruct((n, d), F32),
        compiler_params=_params(1), name="out_proj",
    )(x, oa, ob, w)


FLASH_SLOTS = 3


def _flash_scratch(tk, w, dv):
    n = FLASH_SLOTS
    return ([pltpu.VMEM((tk, w), F32)] * n + [pltpu.VMEM((tk, w), MXU_DTYPE)] * n + [pltpu.VMEM((1, w), F32)] * (2 * n)
            + [pltpu.VMEM((1, w), F32), pltpu.VMEM((1, w), F32), pltpu.VMEM((dv, w), F32)])


def _flash_t(q, k_ref, vt_ref, n_chunks, tk, scratch, bias_fn):
    n = FLASH_SLOTS
    s_bufs, p_bufs, c_bufs, a_bufs = (scratch[i * n:(i + 1) * n] for i in range(4))
    m_ref, l_ref, acc_ref = scratch[4 * n:4 * n + 3]
    assert n_chunks >= 2
    m_ref[...] = jnp.full(m_ref.shape, -jnp.inf, F32)
    l_ref[...] = jnp.zeros(l_ref.shape, F32)
    acc_ref[...] = jnp.zeros(acc_ref.shape, F32)

    def scores(c, slot):
        k = k_ref[pl.ds(pl.multiple_of(c * tk, tk), tk), :]
        st = _nt_dot(k, q)
        if bias_fn is not None:
            st = bias_fn(st, c)
        s_bufs[slot][...] = st
        c_bufs[slot][...] = jnp.max(st, axis=0, keepdims=True)

    def softmax(slot):
        m_prev = m_ref[...]
        m_new = jnp.maximum(m_prev, c_bufs[slot][...])
        p = jnp.exp2(s_bufs[slot][...] - m_new)
        alpha = jnp.exp2(m_prev - m_new)
        l_ref[...] = alpha * l_ref[...] + jnp.sum(p, axis=0, keepdims=True)
        m_ref[...] = m_new
        a_bufs[slot][...] = alpha
        p_bufs[slot][...] = p.astype(MXU_DTYPE)

    def values(c, slot):
        acc_ref[...] = acc_ref[...] * a_bufs[slot][...] + jnp.dot(vt_ref[c], p_bufs[slot][...],
                                                                 preferred_element_type=F32)

    def step(c, slot):
        values(c - 1, (slot - 1) % n)
        scores(c + 1, (slot + 1) % n)
        softmax(slot)

    scores(0, 0)
    scores(1, 1)
    softmax(0)
    steady = n_chunks - 2
    trips = steady // n

    def body(j, carry):
        for r in range(n):
            step(1 + n * j + r, (1 + r) % n)
        return carry

    lax.fori_loop(0, trips, body, 0)
    for c in range(1 + n * trips, n_chunks - 1):
        step(c, c % n)
    values(n_chunks - 2, (n_chunks - 2) % n)
    softmax((n_chunks - 1) % n)
    values(n_chunks - 1, (n_chunks - 1) % n)


def _da_attn_kernel(sc_ref, q_ref, k_ref, vt_ref, g_ref, o_ref, dist_ref, *scratch,
                    tq, tk, n_chunks, n_heads, out_scale):
    l_ref, acc_ref = scratch[4 * FLASH_SLOTS + 1:]
    h, qi = pl.program_id(1), pl.program_id(2)
    slope, lam = sc_ref[h], sc_ref[n_heads]
    q = q_ref[...]
    lane = lax.broadcasted_iota(jnp.int32, q.shape, 1)
    zero = jnp.zeros_like(q)
    d = q.shape[1] // 2
    qs = jnp.concatenate([jnp.where(lane < d, q, zero), jnp.where(lane >= d, q, zero)], axis=0)
    row = lax.broadcasted_iota(jnp.int32, (tk, tq), 0)
    col = lax.broadcasted_iota(jnp.int32, (tk, tq), 1)
    dist_ref[...] = (row - col - qi * tq).astype(F32)

    def bias_fn(st, c):
        b = slope * jnp.abs(dist_ref[...] + lax.convert_element_type(c * tk, F32))
        return jnp.concatenate([st[:, :tq] - b, st[:, tq:] - b], axis=1)

    _flash_t(qs, k_ref, vt_ref, n_chunks, tk, scratch, bias_fn)
    o = acc_ref[...] * (1.0 / l_ref[...])
    o = o[:, :tq] - lam * o[:, tq:]
    o = o * lax.rsqrt(jnp.mean(o * o, axis=0, keepdims=True) + DA_SUBLN_EPS) * g_ref[...] * out_scale
    o_ref[...] = o.T.astype(o_ref.dtype)


def _da_attn(scalars, q, k, vt, subln_col, batch, seq, row0, out_scale):
    dm = q.shape[1]
    tq, tk = DA_Q_TILE, KV_CHUNK
    hw = 2 * DA_HEAD_DIM
    n_heads = dm // hw
    assert row0 % seq == 0 and seq % tq == 0 and seq % tk == 0
    s0, q0, nq, n_chunks = row0 // seq, row0 // tq, seq // tq, seq // tk
    return pl.pallas_call(
        functools.partial(_da_attn_kernel, tq=tq, tk=tk, n_chunks=n_chunks, n_heads=n_heads, out_scale=out_scale),
        grid_spec=pltpu.PrefetchScalarGridSpec(
            num_scalar_prefetch=1, grid=(batch, n_heads, nq),
            in_specs=[pl.BlockSpec((tq, hw), lambda b, h, i, sc: (q0 + b * nq + i, h)),
                      pl.BlockSpec((seq, hw), lambda b, h, i, sc: (s0 + b, h)),
                      pl.BlockSpec((n_chunks, hw, tk), lambda b, h, i, sc: (s0 + b, h, 0)),
                      pl.BlockSpec(subln_col.shape, lambda b, h, i, sc: (0, 0))],
            out_specs=pl.BlockSpec((tq, hw), lambda b, h, i, sc: (b * nq + i, h)),
            scratch_shapes=[pltpu.VMEM((tk, tq), F32)] + _flash_scratch(tk, 2 * tq, hw)),
        out_shape=jax.ShapeDtypeStruct((batch * seq, dm), MXU_DTYPE),
        compiler_params=_params(3), name="da_attn",
    )(scalars, q, k, vt, subln_col)


def _mla_attn_kernel(qn_ref, qr_ref, kn_ref, kr_ref, vt_ref, o_ref, kfull_ref, *scratch, tk, n_chunks):
    l_ref, acc_ref = scratch[4 * FLASH_SLOTS + 1:]

    @pl.when(pl.program_id(2) == 0)
    def _():
        def copy(c, carry):
            rows = pl.ds(pl.multiple_of(c * tk, tk), tk)
            kfull_ref[rows, :LANES] = kn_ref[rows, :]
            kfull_ref[rows, LANES:] = kr_ref[rows, :]
            return carry
        lax.fori_loop(0, n_chunks, copy, 0)

    q = jnp.concatenate([qn_ref[...], qr_ref[...]], axis=1)
    _flash_t(q, kfull_ref, vt_ref, n_chunks, tk, scratch, None)
    o = acc_ref[...] * (1.0 / l_ref[...])
    o_ref[...] = o.T.astype(o_ref.dtype)


def _mla_attn(q, kn, kr, vt, batch, seq, row0):
    n_heads = MLA_HEADS
    tq, tk = MLA_Q_TILE, KV_CHUNK
    assert row0 % seq == 0 and seq % tq == 0 and seq % tk == 0
    s0, q0, nq, n_chunks = row0 // seq, row0 // tq, seq // tq, seq // tk
    return pl.pallas_call(
        functools.partial(_mla_attn_kernel, tk=tk, n_chunks=n_chunks),
        grid=(batch, n_heads, nq),
        in_specs=[pl.BlockSpec((tq, LANES), lambda b, h, i: (q0 + b * nq + i, h)),
                  pl.BlockSpec((tq, LANES), lambda b, h, i: (q0 + b * nq + i, n_heads + h)),
                  pl.BlockSpec((seq, LANES), lambda b, h, i: (s0 + b, h)),
                  pl.BlockSpec((seq, LANES), lambda b, h, i: (s0 + b, 0)),
                  pl.BlockSpec((n_chunks, MLA_V, tk), lambda b, h, i: (s0 + b, h, 0))],
        out_specs=pl.BlockSpec((tq, MLA_V), lambda b, h, i: (b * nq + i, h)),
        out_shape=jax.ShapeDtypeStruct((batch * seq, n_heads * MLA_V), MXU_DTYPE),
        scratch_shapes=[pltpu.VMEM((seq, 2 * LANES), MXU_DTYPE)] + _flash_scratch(tk, tq, MLA_V),
        compiler_params=_params(3), name="mla_attn",
    )(q, q, kn, kr, vt)


R_E0, R_E1, R_RANK0, R_RANK1, R_GATE0, R_GATE1 = range(6)


def _router_kernel(x_ref, g_ref, w_ref, tri_ref, rec_ref, cnt_ref, base_ref):
    i = pl.program_id(0)

    @pl.when(i == 0)
    def _():
        base_ref[...] = jnp.zeros(base_ref.shape, F32)

    xn = _rms(x_ref[...], g_ref[...], NORM_EPS)
    logits = jnp.dot(xn, w_ref[...], preferred_element_type=F32, precision=lax.Precision.HIGHEST)
    tm = logits.shape[0]
    lane = lax.broadcasted_iota(jnp.int32, logits.shape, 1)
    neg = jnp.float32(-jnp.inf)

    def first_lane(mask):
        return jnp.min(jnp.where(mask, lane, LANES), axis=1, keepdims=True)

    is_g = lane < N_GROUPS
    lg = jnp.where(is_g, logits, neg)
    mg = jnp.max(lg, axis=1, keepdims=True)
    pg_sel = 1.0 / jnp.sum(jnp.exp(lg - mg), axis=1, keepdims=True)
    g_sel = first_lane(lg == mg)
    e_lo = N_GROUPS + EXPERTS_PER_GROUP * g_sel
    in_e = (lane >= e_lo) & (lane < e_lo + EXPERTS_PER_GROUP)
    le = jnp.where(in_e, logits, neg)
    me = jnp.max(le, axis=1, keepdims=True)
    ee = jnp.exp(le - me)
    pe = ee / jnp.sum(ee, axis=1, keepdims=True)
    pe = jnp.where(in_e, pe, -1.0)
    p0 = jnp.max(pe, axis=1, keepdims=True)
    i0 = first_lane(pe == p0)
    pe1 = jnp.where(lane == i0, -1.0, pe)
    p1 = jnp.max(pe1, axis=1, keepdims=True)
    i1 = first_lane(pe1 == p1)
    gate0 = pg_sel * p0 / (p0 + p1)
    gate1 = pg_sel * p1 / (p0 + p1)
    hit0, hit1 = lane == i0, lane == i1
    oh = jnp.concatenate([jnp.where(hit0, 1.0, 0.0), jnp.where(hit1, 1.0, 0.0)], axis=1)
    before = jnp.dot(tri_ref[...], oh.astype(MXU_DTYPE), preferred_element_type=F32)
    base = base_ref[...]
    tot0 = jnp.sum(oh[:, :LANES], axis=0, keepdims=True)
    tot1 = jnp.sum(oh[:, LANES:], axis=0, keepdims=True)
    rank0 = jnp.sum(jnp.where(hit0, before[:, :LANES] + base, 0.0), axis=1, keepdims=True)
    rank1 = jnp.sum(jnp.where(hit1, before[:, LANES:] + (base + tot0), 0.0), axis=1, keepdims=True)
    base = base + tot0 + tot1
    base_ref[...] = base
    cnt_ref[...] = base
    rec = jnp.zeros(logits.shape, F32)
    for pos, val in ((R_E0, (i0 - N_GROUPS).astype(F32)), (R_E1, (i1 - N_GROUPS).astype(F32)),
                     (R_RANK0, rank0), (R_RANK1, rank1), (R_GATE0, gate0), (R_GATE1, gate1)):
        rec = jnp.where(lane == pos, val, rec)
    rec_ref[...] = rec


def _router(x, g, w_router, tri):
    n, d = x.shape
    tm = TOKEN_TILE
    full = lambda a: pl.BlockSpec(a.shape, lambda i: (0,) * a.ndim)
    return pl.pallas_call(
        _router_kernel,
        grid=(n // tm,),
        in_specs=[pl.BlockSpec((tm, d), lambda i: (i, 0)), full(g), full(w_router), full(tri)],
        out_specs=[pl.BlockSpec((tm, LANES), lambda i: (i, 0)), pl.BlockSpec((1, LANES), lambda i: (0, 0))],
        out_shape=[jax.ShapeDtypeStruct((n, LANES), F32), jax.ShapeDtypeStruct((1, LANES), F32)],
        scratch_shapes=[pltpu.VMEM((1, LANES), F32)],
        compiler_params=_params(1), name="moe_router",
    )(x, g, w_router, tri)


def _row_copy(src_ref, src_row, dst_ref, dst_row, sem):
    return pltpu.make_async_copy(src_ref.at[pl.ds(src_row, 1), :], dst_ref.at[pl.ds(dst_row, 1), :], sem)


def _dispatch_kernel(dest_ref, x_ref, g_ref, xs_in_ref, xs_ref, xn_ref, sem, *, tm):
    del xs_in_ref
    xn_ref[...] = _rms(x_ref[...], g_ref[...], NORM_EPS)

    def issue(r, carry):
        for c in range(2):
            _row_copy(xn_ref, r, xs_ref, dest_ref[2 * r + c], sem).start()
        return carry
    lax.fori_loop(0, tm, issue, 0)

    def drain(r, carry):
        for c in range(2):
            _row_copy(xn_ref, 0, xs_ref, 0, sem).wait()
        return carry
    lax.fori_loop(0, tm, drain, 0)


def _dispatch(dest_flat, x, g, xs_zero):
    n, d = x.shape
    tm = TOKEN_TILE
    return pl.pallas_call(
        functools.partial(_dispatch_kernel, tm=tm),
        grid=(n // tm,),
        in_specs=[pl.BlockSpec((2 * tm,), lambda i: (i,), memory_space=pltpu.SMEM),
                  pl.BlockSpec((tm, d), lambda i: (i, 0)), pl.BlockSpec(g.shape, lambda i: (0, 0)),
                  pl.BlockSpec(memory_space=pl.ANY)],
        out_specs=pl.BlockSpec(memory_space=pl.ANY),
        out_shape=jax.ShapeDtypeStruct(xs_zero.shape, F32),
        scratch_shapes=[pltpu.VMEM((tm, d), F32), pltpu.SemaphoreType.DMA(())],
        input_output_aliases={3: 0},
        compiler_params=_params(1), name="moe_dispatch",
    )(dest_flat, x, g, xs_zero)


def _expert_kernel(be_ref, nv_ref, xs_ref, w1_ref, w3_ref, w2_ref, y_ref):
    del be_ref

    @pl.when(pl.program_id(0) < nv_ref[0])
    def _():
        xb = xs_ref[...].astype(MXU_DTYPE)
        h1 = jnp.dot(xb, w1_ref[0].astype(MXU_DTYPE), preferred_element_type=F32)
        h3 = jnp.dot(xb, w3_ref[0].astype(MXU_DTYPE), preferred_element_type=F32)
        h = (h1 * jax.nn.sigmoid(h1) * h3).astype(MXU_DTYPE)
        y_ref[...] = jnp.dot(h, w2_ref[0].astype(MXU_DTYPE), preferred_element_type=F32)

    @pl.when(pl.program_id(0) >= nv_ref[0])
    def _():
        y_ref[...] = jnp.zeros(y_ref.shape, F32)


def _experts(block_expert, n_valid, xs, w1, w3, w2):
    p, d = xs.shape
    rb = EXPERT_ROWS
    de = w1.shape[2]
    return pl.pallas_call(
        _expert_kernel,
        grid_spec=pltpu.PrefetchScalarGridSpec(
            num_scalar_prefetch=2, grid=(p // rb,),
            in_specs=[pl.BlockSpec((rb, d), lambda b, be, nv: (b, 0)),
                      pl.BlockSpec((1, d, de), lambda b, be, nv: (be[b], 0, 0)),
                      pl.BlockSpec((1, d, de), lambda b, be, nv: (be[b], 0, 0)),
                      pl.BlockSpec((1, de, d), lambda b, be, nv: (be[b], 0, 0))],
            out_specs=pl.BlockSpec((rb, d), lambda b, be, nv: (b, 0))),
        out_shape=jax.ShapeDtypeStruct((p, d), F32),
        compiler_params=_params(1), name="moe_experts",
    )(block_expert, n_valid, xs, w1, w3, w2)


def _combine_kernel(dest_ref, x_ref, rec_ref, gfin_ref, yb_ref, o_ref, ybuf_ref, sem, *, tm, final):
    def issue(r, carry):
        for c in range(2):
            _row_copy(yb_ref, dest_ref[2 * r + c], ybuf_ref.at[c], r, sem).start()
        return carry
    lax.fori_loop(0, tm, issue, 0)

    def drain(r, carry):
        for c in range(2):
            _row_copy(yb_ref, 0, ybuf_ref.at[c], 0, sem).wait()
        return carry
    lax.fori_loop(0, tm, drain, 0)

    rec = rec_ref[...]
    out = (x_ref[...] + rec[:, R_GATE0:R_GATE0 + 1] * ybuf_ref[0] + rec[:, R_GATE1:R_GATE1 + 1] * ybuf_ref[1])
    if final:
        out = _rms(out, gfin_ref[...], NORM_EPS)
    o_ref[...] = out


def _combine(dest_flat, x, rec, g_final, yb, final):
    n, d = x.shape
    tm = TOKEN_TILE
    return pl.pallas_call(
        functools.partial(_combine_kernel, tm=tm, final=final),
        grid=(n // tm,),
        in_specs=[pl.BlockSpec((2 * tm,), lambda i: (i,), memory_space=pltpu.SMEM),
                  pl.BlockSpec((tm, d), lambda i: (i, 0)), pl.BlockSpec((tm, LANES), lambda i: (i, 0)),
                  pl.BlockSpec(g_final.shape, lambda i: (0, 0)), pl.BlockSpec(memory_space=pl.ANY)],
        out_specs=pl.BlockSpec((tm, d), lambda i: (i, 0)),
        out_shape=jax.ShapeDtypeStruct((n, d), F32),
        scratch_shapes=[pltpu.VMEM((2, tm, d), F32), pltpu.SemaphoreType.DMA(())],
        compiler_params=_params(1), name="moe_combine",
    )(dest_flat, x, rec, g_final, yb)


def _moe(x, g, w_router, tri, w1, w3, w2, g_final, final):
    n, d = x.shape
    rb = EXPERT_ROWS
    rec, counts = _router(x, g, w_router, tri)
    counts = counts[0, N_GROUPS:N_GROUPS + N_EXPERTS].astype(jnp.int32)
    padded = (counts + rb - 1) // rb * rb
    pad_end = jnp.cumsum(padded)
    pad_start = pad_end - padded
    n_blocks = (2 * n + N_EXPERTS * (rb - 1)) // rb
    expert = rec[:, R_E0:R_E1 + 1].astype(jnp.int32)
    rank = rec[:, R_RANK0:R_RANK1 + 1].astype(jnp.int32)
    dest = (pad_start[expert] + rank).reshape(-1)
    block_start = jnp.arange(n_blocks, dtype=jnp.int32) * rb
    block_expert = jnp.minimum(jnp.searchsorted(pad_end, block_start, side="right"), N_EXPERTS - 1).astype(jnp.int32)
    n_valid = (pad_end[-1:] // rb).astype(jnp.int32)
    xs = _dispatch(dest, x, g, jnp.zeros((n_blocks * rb, d), F32))
    yb = _experts(block_expert, n_valid, xs, w1, w3, w2)
    return _combine(dest, x, rec, g_final, yb, final)


def _rope_tables(positions):
    inv = 1.0 / (ROPE_THETA ** (jnp.arange(0, MLA_ROPE, 2, dtype=F32) / MLA_ROPE))
    ang = positions.astype(F32)[:, None] * inv[None, :]
    cos = jnp.repeat(jnp.cos(ang), 2, axis=1)
    sin = jnp.stack([-jnp.sin(ang), jnp.sin(ang)], axis=-1).reshape(ang.shape[0], MLA_ROPE)
    pad = ((0, 0), (0, LANES - MLA_ROPE))
    return jnp.pad(cos, pad), jnp.pad(sin, pad)


def _pair_swap(w):
    return w.reshape(w.shape[0], -1, 2)[:, :, ::-1].reshape(w.shape)


def kernel(x_prompt, x_sample, ln_mix, ln_ffn, ln_final, da_wqkv, da_wo, da_lambda_q1, da_lambda_k1, da_lambda_q2, da_lambda_k2, da_subln, mla_w_down, mla_q_norm, mla_w_uq, mla_kv_norm, mla_w_ukv, mla_wo, moe_w_group, moe_w_expert, moe_w1, moe_w3, moe_w2):
    bp, sp, d = x_prompt.shape
    bs, ss, _ = x_sample.shape
    n_p, n_s = bp * sp, bs * ss
    groups = ((bp, sp, 0), (bs, ss, n_p))
    x = jnp.concatenate([x_prompt.reshape(n_p, d), x_sample.reshape(n_s, d)], axis=0)
    depth = ln_mix.shape[0]
    cdt = MXU_DTYPE

    positions = jnp.concatenate([jnp.tile(jnp.arange(sp), bp), jnp.tile(jnp.arange(ss), bs)])
    cos, sin = _rope_tables(positions)
    tri = jnp.tril(jnp.ones((TOKEN_TILE, TOKEN_TILE), F32), -1).astype(cdt)
    n_da_heads = d // (2 * DA_HEAD_DIM)
    slopes = 2.0 ** (-8.0 * np.arange(1, n_da_heads + 1) / n_da_heads) * LOG2E
    hd = MLA_HEADS * LANES
    zpad = LANES - MLA_ROPE

    for i in range(depth):
        j = i // 2
        g_mix = ln_mix[i][None, :]
        if i % 2 == 0:
            lambda_init = 0.8 - 0.6 * math.exp(-0.3 * i)
            lam = (jnp.exp(jnp.sum(da_lambda_q1[j] * da_lambda_k1[j])) - jnp.exp(jnp.sum(da_lambda_q2[j] * da_lambda_k2[j]))
                   + lambda_init)
            scalars = jnp.concatenate([jnp.asarray(slopes, F32), lam[None].astype(F32)])
            w = da_wqkv[j]
            q, k, vt = _da_qkv(x, g_mix, w[:, :d].astype(cdt), w[:, d:2 * d].astype(cdt), w[:, 2 * d:].T.astype(cdt),
                               DA_HEAD_DIM ** -0.5 * LOG2E)
            oa, ob = [_da_attn(scalars, q, k, vt, da_subln[j][:, None], batch, seq, row0, 1.0 - lambda_init)
                      for batch, seq, row0 in groups]
            x = _out_proj(x, oa, ob, da_wo[j].astype(cdt))
        else:
            qr, kvr = MLA_Q_RANK, MLA_KV_RANK
            wd = mla_w_down[j]
            w_rope = wd[:, qr + kvr:]
            wd = jnp.concatenate([wd[:, :qr + kvr], jnp.pad(w_rope, ((0, 0), (0, zpad))),
                                  jnp.pad(_pair_swap(w_rope), ((0, 0), (0, zpad)))], axis=1)
            wuq = mla_w_uq[j].reshape(qr, MLA_HEADS, MLA_NOPE + MLA_ROPE)
            wq_rope = wuq[:, :, MLA_NOPE:]
            pad3 = ((0, 0), (0, 0), (0, zpad))
            wuq = jnp.concatenate([wuq[:, :, :MLA_NOPE].reshape(qr, hd), jnp.pad(wq_rope, pad3).reshape(qr, hd),
                                   jnp.pad(_pair_swap(wq_rope.reshape(qr, -1)).reshape(wq_rope.shape), pad3).reshape(qr, hd)],
                                  axis=1)
            wukv = mla_w_ukv[j].reshape(kvr, MLA_HEADS, MLA_NOPE + MLA_V)
            wkn = wukv[:, :, :MLA_NOPE].reshape(kvr, hd)
            wvt = wukv[:, :, MLA_NOPE:].reshape(kvr, MLA_HEADS * MLA_V).T
            q, kn, kr, vt = _mla_proj(x, g_mix, wd.astype(cdt), mla_q_norm[j][None, :], mla_kv_norm[j][None, :],
                                      wuq.astype(cdt), wkn.astype(cdt), wvt.astype(cdt), cos, sin,
                                      (MLA_NOPE + MLA_ROPE) ** -0.5 * LOG2E)
            oa, ob = [_mla_attn(q, kn, kr, vt, batch, seq, row0) for batch, seq, row0 in groups]
            x = _out_proj(x, oa, ob, mla_wo[j].astype(cdt))
        w_router = jnp.pad(jnp.concatenate([moe_w_group[i], moe_w_expert[i]], axis=1),
                           ((0, 0), (0, LANES - N_GROUPS - N_EXPERTS)))
        x = _moe(x, ln_ffn[i][None, :], w_router, tri, moe_w1[i], moe_w3[i], moe_w2[i], ln_final[None, :],
                 final=(i == depth - 1))
    return x[:n_p].reshape(bp, sp, d), x[n_p:].reshape(bs, ss, d)
```

```python
import functools
import math

import numpy as np
import jax
import jax.numpy as jnp
from jax import lax
from jax.experimental import pallas as pl
from jax.experimental.pallas import tpu as pltpu

F32 = jnp.float32
MXU_DTYPE = jnp.bfloat16
LOG2E = 1.4426950408889634

NORM_EPS = 1e-6
DA_SUBLN_EPS = 1e-5
ROPE_THETA = 10000.0
DA_HEAD_DIM = 64
MLA_HEADS = 8
MLA_NOPE = 128
MLA_ROPE = 64
MLA_V = 128
MLA_Q_RANK = 384
MLA_KV_RANK = 256
N_GROUPS = 8
EXPERTS_PER_GROUP = 8
N_EXPERTS = N_GROUPS * EXPERTS_PER_GROUP

LANES = 128
TOKEN_TILE = 512
KV_CHUNK = 512
DA_Q_TILE = 256
MLA_Q_TILE = 512
MLA_COLS = 256
EXPERT_ROWS = 256
VMEM_LIMIT = 56 * 1024 * 1024


def _rms(x, g, eps):
    return x * lax.rsqrt(jnp.mean(x * x, axis=-1, keepdims=True) + eps) * g


def _nt_dot(a, b):
    return lax.dot_general(a, b, (((1,), (1,)), ((), ())), preferred_element_type=F32)


def _params(n_axes):
    return pltpu.CompilerParams(dimension_semantics=("arbitrary",) * n_axes, vmem_limit_bytes=VMEM_LIMIT)


def _da_qkv_kernel(x_ref, g_ref, wq_ref, wk_ref, wvt_ref, q_ref, k_ref, vt_ref, *, q_scale):
    xn = _rms(x_ref[...], g_ref[...], NORM_EPS).astype(MXU_DTYPE)
    q_ref[...] = (jnp.dot(xn, wq_ref[...], preferred_element_type=F32) * q_scale).astype(q_ref.dtype)
    k_ref[...] = jnp.dot(xn, wk_ref[...], preferred_element_type=F32).astype(k_ref.dtype)
    vt_ref[0] = _nt_dot(wvt_ref[...], xn).astype(vt_ref.dtype)


def _da_qkv(x, g, wq, wk, wvt, q_scale):
    n, d = x.shape
    tm = TOKEN_TILE
    full = lambda a: pl.BlockSpec(a.shape, lambda i: (0,) * a.ndim)
    return pl.pallas_call(
        functools.partial(_da_qkv_kernel, q_scale=q_scale),
        grid=(n // tm,),
        in_specs=[pl.BlockSpec((tm, d), lambda i: (i, 0)), full(g), full(wq), full(wk), full(wvt)],
        out_specs=[pl.BlockSpec((tm, d), lambda i: (i, 0)), pl.BlockSpec((tm, d), lambda i: (i, 0)),
                   pl.BlockSpec((1, d, tm), lambda i: (i, 0, 0))],
        out_shape=[jax.ShapeDtypeStruct((n, d), MXU_DTYPE), jax.ShapeDtypeStruct((n, d), MXU_DTYPE),
                   jax.ShapeDtypeStruct((n // tm, d, tm), MXU_DTYPE)],
        compiler_params=_params(1), name="da_qkv",
    )(x, g, wq, wk, wvt)


def _mla_proj_kernel(x_ref, g_ref, wd_ref, qg_ref, kvg_ref, wuq_ref, wkn_ref, wvt_ref, cos_ref, sin_ref,
                     q_ref, kn_ref, kr_ref, vt_ref, *, q_scale):
    xn = _rms(x_ref[...], g_ref[...], NORM_EPS).astype(MXU_DTYPE)
    down = jnp.dot(xn, wd_ref[...], preferred_element_type=F32)
    qr, kvr = MLA_Q_RANK, MLA_KV_RANK
    c_q = _rms(down[:, :qr], qg_ref[...], NORM_EPS).astype(MXU_DTYPE)
    c_kv = _rms(down[:, qr:qr + kvr], kvg_ref[...], NORM_EPS).astype(MXU_DTYPE)
    cos, sin = cos_ref[...], sin_ref[...]
    kr = down[:, qr + kvr:qr + kvr + LANES] * cos + down[:, qr + kvr + LANES:] * sin
    kr_ref[...] = kr.astype(kr_ref.dtype)
    hd = MLA_HEADS * LANES
    q = jnp.dot(c_q, wuq_ref[...], preferred_element_type=F32)
    q_ref[:, :hd] = (q[:, :hd] * q_scale).astype(q_ref.dtype)
    for h in range(MLA_HEADS):
        a = q[:, hd + h * LANES:hd + (h + 1) * LANES]
        b = q[:, 2 * hd + h * LANES:2 * hd + (h + 1) * LANES]
        q_ref[:, hd + h * LANES:hd + (h + 1) * LANES] = ((a * cos + b * sin) * q_scale).astype(q_ref.dtype)
    kn_ref[...] = jnp.dot(c_kv, wkn_ref[...], preferred_element_type=F32).astype(kn_ref.dtype)
    vt_ref[0] = _nt_dot(wvt_ref[...], c_kv).astype(vt_ref.dtype)


def _mla_proj(x, g, wd, qg, kvg, wuq, wkn, wvt, cos, sin, q_scale):
    n, d = x.shape
    tm = TOKEN_TILE
    hd = MLA_HEADS * LANES
    full = lambda a: pl.BlockSpec(a.shape, lambda i: (0,) * a.ndim)
    row = lambda w: pl.BlockSpec((tm, w), lambda i: (i, 0))
    return pl.pallas_call(
        functools.partial(_mla_proj_kernel, q_scale=q_scale),
        grid=(n // tm,),
        in_specs=[row(d), full(g), full(wd), full(qg), full(kvg), full(wuq), full(wkn), full(wvt),
                  row(LANES), row(LANES)],
        out_specs=[row(2 * hd), row(hd), row(LANES), pl.BlockSpec((1, hd, tm), lambda i: (i, 0, 0))],
        out_shape=[jax.ShapeDtypeStruct((n, 2 * hd), MXU_DTYPE), jax.ShapeDtypeStruct((n, hd), MXU_DTYPE),
                   jax.ShapeDtypeStruct((n, LANES), MXU_DTYPE),
                   jax.ShapeDtypeStruct((n // tm, hd, tm), MXU_DTYPE)],
        compiler_params=_params(1), name="mla_proj",
    )(x, g, wd, qg, kvg, wuq, wkn, wvt, cos, sin)


def _out_proj_kernel(x_ref, oa_ref, ob_ref, w_ref, y_ref, *, a_tiles):
    @pl.when(pl.program_id(0) < a_tiles)
    def _():
        y_ref[...] = x_ref[...] + jnp.dot(oa_ref[...], w_ref[...], preferred_element_type=F32)

    @pl.when(pl.program_id(0) >= a_tiles)
    def _():
        y_ref[...] = x_ref[...] + jnp.dot(ob_ref[...], w_ref[...], preferred_element_type=F32)


def _out_proj(x, oa, ob, w):
    n, d = x.shape
    tm = TOKEN_TILE
    a_tiles = oa.shape[0] // tm
    assert oa.shape[0] % tm == 0 and ob.shape[0] % tm == 0 and oa.shape[0] + ob.shape[0] == n
    return pl.pallas_call(
        functools.partial(_out_proj_kernel, a_tiles=a_tiles),
        grid=(n // tm,),
        in_specs=[pl.BlockSpec((tm, d), lambda i: (i, 0)),
                  pl.BlockSpec((tm, oa.shape[1]), lambda i: (jnp.minimum(i, a_tiles - 1), 0)),
                  pl.BlockSpec((tm, ob.shape[1]), lambda i: (jnp.maximum(i - a_tiles, 0), 0)),
                  pl.BlockSpec(w.shape, lambda i: (0, 0))],
        out_specs=pl.BlockSpec((tm, d), lambda i: (i, 0)),
        out_shape=jax.ShapeDtypeStruct((n, d), F32),
        compiler_params=_params(1), name="out_proj",
    )(x, oa, ob, w)


FLASH_STRIP = 32
FLASH_SLOTS = 3


def _flash_scratch(tk, w, dv):
    n = FLASH_SLOTS
    return ([pltpu.VMEM((tk, w), F32)] * n + [pltpu.VMEM((tk, w), MXU_DTYPE)] * n + [pltpu.VMEM((1, w), F32)] * (2 * n)
            + [pltpu.VMEM((1, w), F32), pltpu.VMEM((1, w), F32), pltpu.VMEM((dv, w), F32)])


def _flash_t(q, k_ref, vt_ref, n_chunks, tk, scratch, bias_fn, cols):
    n = FLASH_SLOTS
    s_bufs, p_bufs, c_bufs, a_bufs = (scratch[i * n:(i + 1) * n] for i in range(4))
    m_ref, l_ref, acc_ref = scratch[4 * n:4 * n + 3]
    assert n_chunks >= 2
    m_ref[...] = jnp.full(m_ref.shape, -jnp.inf, F32)
    l_ref[...] = jnp.zeros(l_ref.shape, F32)
    acc_ref[...] = jnp.zeros(acc_ref.shape, F32)

    w = q.shape[0]
    groups = [slice(g, g + cols) for g in range(0, w, cols)]

    def scores_g(c, slot, g):
        k = k_ref[pl.ds(pl.multiple_of(c * tk, tk), tk), :]
        st = _nt_dot(k, q[g, :])
        if bias_fn is not None:
            st = bias_fn(st, c)
        cmax = None
        for r in range(0, tk, FLASH_STRIP):
            strip = st[r:r + FLASH_STRIP, :]
            s_bufs[slot][r:r + FLASH_STRIP, g] = strip
            cmax = strip if cmax is None else jnp.maximum(cmax, strip)
        c_bufs[slot][:, g] = jnp.max(cmax, axis=0, keepdims=True)

    def softmax_g(slot, g):
        m_prev = m_ref[:, g]
        m_new = jnp.maximum(m_prev, c_bufs[slot][:, g])
        alpha = jnp.exp2(m_prev - m_new)
        m_ref[:, g] = m_new
        a_bufs[slot][:, g] = alpha
        psum = None
        for r in range(0, tk, FLASH_STRIP):
            p = jnp.exp2(s_bufs[slot][r:r + FLASH_STRIP, g] - m_new)
            psum = p if psum is None else psum + p
            p_bufs[slot][r:r + FLASH_STRIP, g] = p.astype(MXU_DTYPE)
        l_ref[:, g] = alpha * l_ref[:, g] + jnp.sum(psum, axis=0, keepdims=True)

    def values_g(c, slot, g):
        acc_ref[:, g] = acc_ref[:, g] * a_bufs[slot][:, g] + jnp.dot(vt_ref[c], p_bufs[slot][:, g],
                                                                     preferred_element_type=F32)

    def scores(c, slot):
        for g in groups:
            scores_g(c, slot, g)

    def softmax(slot):
        for g in groups:
            softmax_g(slot, g)

    def values(c, slot):
        for g in groups:
            values_g(c, slot, g)

    def step(c, slot):
        for g in groups:
            values_g(c - 1, (slot - 1) % n, g)
            scores_g(c + 1, (slot + 1) % n, g)
            softmax_g(slot, g)

    scores(0, 0)
    scores(1, 1)
    softmax(0)
    steady = n_chunks - 2
    trips = steady // n

    def body(j, carry):
        for r in range(n):
            step(1 + n * j + r, (1 + r) % n)
        return carry

    lax.fori_loop(0, trips, body, 0)
    for c in range(1 + n * trips, n_chunks - 1):
        step(c, c % n)
    values(n_chunks - 2, (n_chunks - 2) % n)
    softmax((n_chunks - 1) % n)
    values(n_chunks - 1, (n_chunks - 1) % n)


def _da_attn_kernel(sc_ref, q_ref, k_ref, vt_ref, g_ref, o_ref, dist_ref, *scratch,
                    tq, tk, n_chunks, n_heads, out_scale):
    l_ref, acc_ref = scratch[4 * FLASH_SLOTS + 1:]
    h, qi = pl.program_id(1), pl.program_id(2)
    slope, lam = sc_ref[h], sc_ref[n_heads]
    q = q_ref[...]
    lane = lax.broadcasted_iota(jnp.int32, q.shape, 1)
    zero = jnp.zeros_like(q)
    d = q.shape[1] // 2
    qs = jnp.concatenate([jnp.where(lane < d, q, zero), jnp.where(lane >= d, q, zero)], axis=0)
    row = lax.broadcasted_iota(jnp.int32, (tk, tq), 0)
    col = lax.broadcasted_iota(jnp.int32, (tk, tq), 1)
    dist_ref[...] = (row - col - qi * tq).astype(F32)

    def bias_fn(st, c):
        b = slope * jnp.abs(dist_ref[...] + lax.convert_element_type(c * tk, F32))
        return jnp.concatenate([st[:, :tq] - b, st[:, tq:] - b], axis=1)

    _flash_t(qs, k_ref, vt_ref, n_chunks, tk, scratch, bias_fn, 2 * tq)
    o = acc_ref[...] * (1.0 / l_ref[...])
    o = o[:, :tq] - lam * o[:, tq:]
    o = o * lax.rsqrt(jnp.mean(o * o, axis=0, keepdims=True) + DA_SUBLN_EPS) * g_ref[...] * out_scale
    o_ref[...] = o.T.astype(o_ref.dtype)


def _da_attn(scalars, q, k, vt, subln_col, batch, seq, row0, out_scale):
    dm = q.shape[1]
    tq, tk = DA_Q_TILE, KV_CHUNK
    hw = 2 * DA_HEAD_DIM
    n_heads = dm // hw
    assert row0 % seq == 0 and seq % tq == 0 and seq % tk == 0
    s0, q0, nq, n_chunks = row0 // seq, row0 // tq, seq // tq, seq // tk
    return pl.pallas_call(
        functools.partial(_da_attn_kernel, tq=tq, tk=tk, n_chunks=n_chunks, n_heads=n_heads, out_scale=out_scale),
        grid_spec=pltpu.PrefetchScalarGridSpec(
            num_scalar_prefetch=1, grid=(batch, n_heads, nq),
            in_specs=[pl.BlockSpec((tq, hw), lambda b, h, i, sc: (q0 + b * nq + i, h)),
                      pl.BlockSpec((seq, hw), lambda b, h, i, sc: (s0 + b, h)),
                      pl.BlockSpec((n_chunks, hw, tk), lambda b, h, i, sc: (s0 + b, h, 0)),
                      pl.BlockSpec(subln_col.shape, lambda b, h, i, sc: (0, 0))],
            out_specs=pl.BlockSpec((tq, hw), lambda b, h, i, sc: (b * nq + i, h)),
            scratch_shapes=[pltpu.VMEM((tk, tq), F32)] + _flash_scratch(tk, 2 * tq, hw)),
        out_shape=jax.ShapeDtypeStruct((batch * seq, dm), MXU_DTYPE),
        compiler_params=_params(3), name="da_attn",
    )(scalars, q, k, vt, subln_col)


def _mla_attn_kernel(qn_ref, qr_ref, kn_ref, kr_ref, vt_ref, o_ref, kfull_ref, *scratch, tk, n_chunks):
    l_ref, acc_ref = scratch[4 * FLASH_SLOTS + 1:]

    @pl.when(pl.program_id(2) == 0)
    def _():
        def copy(c, carry):
            rows = pl.ds(pl.multiple_of(c * tk, tk), tk)
            kfull_ref[rows, :LANES] = kn_ref[rows, :]
            kfull_ref[rows, LANES:] = kr_ref[rows, :]
            return carry
        lax.fori_loop(0, n_chunks, copy, 0)

    q = jnp.concatenate([qn_ref[...], qr_ref[...]], axis=1)
    _flash_t(q, kfull_ref, vt_ref, n_chunks, tk, scratch, None, MLA_COLS)
    o = acc_ref[...] * (1.0 / l_ref[...])
    o_ref[...] = o.T.astype(o_ref.dtype)


def _mla_attn(q, kn, kr, vt, batch, seq, row0):
    n_heads = MLA_HEADS
    tq, tk = MLA_Q_TILE, KV_CHUNK
    assert row0 % seq == 0 and seq % tq == 0 and seq % tk == 0
    s0, q0, nq, n_chunks = row0 // seq, row0 // tq, seq // tq, seq // tk
    return pl.pallas_call(
        functools.partial(_mla_attn_kernel, tk=tk, n_chunks=n_chunks),
        grid=(batch, n_heads, nq),
        in_specs=[pl.BlockSpec((tq, LANES), lambda b, h, i: (q0 + b * nq + i, h)),
                  pl.BlockSpec((tq, LANES), lambda b, h, i: (q0 + b * nq + i, n_heads + h)),
                  pl.BlockSpec((seq, LANES), lambda b, h, i: (s0 + b, h)),
                  pl.BlockSpec((seq, LANES), lambda b, h, i: (s0 + b, 0)),
                  pl.BlockSpec((n_chunks, MLA_V, tk), lambda b, h, i: (s0 + b, h, 0))],
        out_specs=pl.BlockSpec((tq, MLA_V), lambda b, h, i: (b * nq + i, h)),
        out_shape=jax.ShapeDtypeStruct((batch * seq, n_heads * MLA_V), MXU_DTYPE),
        scratch_shapes=[pltpu.VMEM((seq, 2 * LANES), MXU_DTYPE)] + _flash_scratch(tk, tq, MLA_V),
        compiler_params=_params(3), name="mla_attn",
    )(q, q, kn, kr, vt)


R_E0, R_E1, R_RANK0, R_RANK1, R_GATE0, R_GATE1 = range(6)


def _router_kernel(x_ref, g_ref, w_ref, tri_ref, rec_ref, cnt_ref, base_ref):
    i = pl.program_id(0)

    @pl.when(i == 0)
    def _():
        base_ref[...] = jnp.zeros(base_ref.shape, F32)

    xn = _rms(x_ref[...], g_ref[...], NORM_EPS)
    logits = jnp.dot(xn, w_ref[...], preferred_element_type=F32, precision=lax.Precision.HIGHEST)
    tm = logits.shape[0]
    lane = lax.broadcasted_iota(jnp.int32, logits.shape, 1)
    neg = jnp.float32(-jnp.inf)

    def first_lane(mask):
        return jnp.min(jnp.where(mask, lane, LANES), axis=1, keepdims=True)

    is_g = lane < N_GROUPS
    lg = jnp.where(is_g, logits, neg)
    mg = jnp.max(lg, axis=1, keepdims=True)
    pg_sel = 1.0 / jnp.sum(jnp.exp(lg - mg), axis=1, keepdims=True)
    g_sel = first_lane(lg == mg)
    e_lo = N_GROUPS + EXPERTS_PER_GROUP * g_sel
    in_e = (lane >= e_lo) & (lane < e_lo + EXPERTS_PER_GROUP)
    le = jnp.where(in_e, logits, neg)
    me = jnp.max(le, axis=1, keepdims=True)
    ee = jnp.exp(le - me)
    pe = ee / jnp.sum(ee, axis=1, keepdims=True)
    pe = jnp.where(in_e, pe, -1.0)
    p0 = jnp.max(pe, axis=1, keepdims=True)
    i0 = first_lane(pe == p0)
    pe1 = jnp.where(lane == i0, -1.0, pe)
    p1 = jnp.max(pe1, axis=1, keepdims=True)
    i1 = first_lane(pe1 == p1)
    gate0 = pg_sel * p0 / (p0 + p1)
    gate1 = pg_sel * p1 / (p0 + p1)
    hit0, hit1 = lane == i0, lane == i1
    oh = jnp.concatenate([jnp.where(hit0, 1.0, 0.0), jnp.where(hit1, 1.0, 0.0)], axis=1)
    before = jnp.dot(tri_ref[...], oh.astype(MXU_DTYPE), preferred_element_type=F32)
    base = base_ref[...]
    tot0 = jnp.sum(oh[:, :LANES], axis=0, keepdims=True)
    tot1 = jnp.sum(oh[:, LANES:], axis=0, keepdims=True)
    rank0 = jnp.sum(jnp.where(hit0, before[:, :LANES] + base, 0.0), axis=1, keepdims=True)
    rank1 = jnp.sum(jnp.where(hit1, before[:, LANES:] + (base + tot0), 0.0), axis=1, keepdims=True)
    base = base + tot0 + tot1
    base_ref[...] = base
    cnt_ref[...] = base
    rec = jnp.zeros(logits.shape, F32)
    for pos, val in ((R_E0, (i0 - N_GROUPS).astype(F32)), (R_E1, (i1 - N_GROUPS).astype(F32)),
                     (R_RANK0, rank0), (R_RANK1, rank1), (R_GATE0, gate0), (R_GATE1, gate1)):
        rec = jnp.where(lane == pos, val, rec)
    rec_ref[...] = rec


def _router(x, g, w_router, tri):
    n, d = x.shape
    tm = TOKEN_TILE
    full = lambda a: pl.BlockSpec(a.shape, lambda i: (0,) * a.ndim)
    return pl.pallas_call(
        _router_kernel,
        grid=(n // tm,),
        in_specs=[pl.BlockSpec((tm, d), lambda i: (i, 0)), full(g), full(w_router), full(tri)],
        out_specs=[pl.BlockSpec((tm, LANES), lambda i: (i, 0)), pl.BlockSpec((1, LANES), lambda i: (0, 0))],
        out_shape=[jax.ShapeDtypeStruct((n, LANES), F32), jax.ShapeDtypeStruct((1, LANES), F32)],
        scratch_shapes=[pltpu.VMEM((1, LANES), F32)],
        compiler_params=_params(1), name="moe_router",
    )(x, g, w_router, tri)


ROW_TILE = 8


def _row_copy(src_ref, src_row, dst_ref, dst_row, sem):
    src = src_ref.at[pl.ds(pl.multiple_of(src_row * ROW_TILE, ROW_TILE), ROW_TILE), :]
    dst = dst_ref.at[pl.ds(pl.multiple_of(dst_row * ROW_TILE, ROW_TILE), ROW_TILE), :]
    return pltpu.make_async_copy(src, dst, sem)


def _to_row_tiles(ref, x):
    rows = x.shape[0]
    for c in range(ROW_TILE):
        ref[pl.ds(c, rows, stride=ROW_TILE), :] = x[:, c * LANES:(c + 1) * LANES]


def _from_row_tiles(ref):
    rows = ref.shape[0] // ROW_TILE
    return jnp.concatenate([ref[pl.ds(c, rows, stride=ROW_TILE), :] for c in range(ROW_TILE)], axis=1)


def _dispatch_kernel(dest_ref, x_ref, g_ref, xs_in_ref, xs_ref, xn_ref, sem, *, tm):
    del xs_in_ref
    _to_row_tiles(xn_ref, _rms(x_ref[...], g_ref[...], NORM_EPS))

    def issue(r, carry):
        for c in range(2):
            _row_copy(xn_ref, r, xs_ref, dest_ref[2 * r + c], sem).start()
        return carry
    lax.fori_loop(0, tm, issue, 0, unroll=8)

    def drain(r, carry):
        for c in range(2):
            _row_copy(xn_ref, 0, xs_ref, 0, sem).wait()
        return carry
    lax.fori_loop(0, tm, drain, 0, unroll=8)


def _dispatch(dest_flat, x, g, xs_zero):
    n, d = x.shape
    tm = TOKEN_TILE
    return pl.pallas_call(
        functools.partial(_dispatch_kernel, tm=tm),
        grid=(n // tm,),
        in_specs=[pl.BlockSpec((2 * tm,), lambda i: (i,), memory_space=pltpu.SMEM),
                  pl.BlockSpec((tm, d), lambda i: (i, 0)), pl.BlockSpec(g.shape, lambda i: (0, 0)),
                  pl.BlockSpec(memory_space=pl.ANY)],
        out_specs=pl.BlockSpec(memory_space=pl.ANY),
        out_shape=jax.ShapeDtypeStruct(xs_zero.shape, F32),
        scratch_shapes=[pltpu.VMEM((tm * ROW_TILE, LANES), F32), pltpu.SemaphoreType.DMA(())],
        input_output_aliases={3: 0},
        compiler_params=_params(1), name="moe_dispatch",
    )(dest_flat, x, g, xs_zero)


def _expert_kernel(be_ref, nv_ref, xs_ref, w1_ref, w3_ref, w2_ref, y_ref, w1c_ref, w3c_ref, w2c_ref):
    b = pl.program_id(0)

    @pl.when((b == 0) | (be_ref[b] != be_ref[jnp.maximum(b - 1, 0)]))
    def _():
        w1c_ref[...] = w1_ref[0, 0].astype(MXU_DTYPE)
        w3c_ref[...] = w3_ref[0, 0].astype(MXU_DTYPE)
        w2c_ref[...] = w2_ref[0, 0].astype(MXU_DTYPE)

    @pl.when(b < nv_ref[0])
    def _():
        xb = _from_row_tiles(xs_ref).astype(MXU_DTYPE)
        h1 = jnp.dot(xb, w1c_ref[...], preferred_element_type=F32)
        h3 = jnp.dot(xb, w3c_ref[...], preferred_element_type=F32)
        h = (h1 * jax.nn.sigmoid(h1) * h3).astype(MXU_DTYPE)
        _to_row_tiles(y_ref, jnp.dot(h, w2c_ref[...], preferred_element_type=F32))

    @pl.when(b >= nv_ref[0])
    def _():
        y_ref[...] = jnp.zeros(y_ref.shape, F32)


def _experts(block_expert, n_valid, xs, w1, w3, w2, layer):
    rb = EXPERT_ROWS
    d, de = w1.shape[2:]
    assert d == ROW_TILE * LANES
    return pl.pallas_call(
        _expert_kernel,
        grid_spec=pltpu.PrefetchScalarGridSpec(
            num_scalar_prefetch=2, grid=(xs.shape[0] // (rb * ROW_TILE),),
            in_specs=[pl.BlockSpec((rb * ROW_TILE, LANES), lambda b, be, nv: (b, 0)),
                      pl.BlockSpec((1, 1, d, de), lambda b, be, nv: (layer, be[b], 0, 0)),
                      pl.BlockSpec((1, 1, d, de), lambda b, be, nv: (layer, be[b], 0, 0)),
                      pl.BlockSpec((1, 1, de, d), lambda b, be, nv: (layer, be[b], 0, 0))],
            out_specs=pl.BlockSpec((rb * ROW_TILE, LANES), lambda b, be, nv: (b, 0)),
            scratch_shapes=[pltpu.VMEM((d, de), MXU_DTYPE), pltpu.VMEM((d, de), MXU_DTYPE),
                            pltpu.VMEM((de, d), MXU_DTYPE)]),
        out_shape=jax.ShapeDtypeStruct(xs.shape, F32),
        compiler_params=_params(1), name="moe_experts",
    )(block_expert, n_valid, xs, w1, w3, w2)


def _combine_kernel(dest_ref, x_ref, rec_ref, gfin_ref, yb_ref, o_ref, ybuf_ref, sem, *, tm, final):
    def issue(r, carry):
        for c in range(2):
            _row_copy(yb_ref, dest_ref[2 * r + c], ybuf_ref.at[c], r, sem).start()
        return carry
    lax.fori_loop(0, tm, issue, 0, unroll=8)

    def drain(r, carry):
        for c in range(2):
            _row_copy(yb_ref, 0, ybuf_ref.at[c], 0, sem).wait()
        return carry
    lax.fori_loop(0, tm, drain, 0, unroll=8)

    rec = rec_ref[...]
    out = (x_ref[...] + rec[:, R_GATE0:R_GATE0 + 1] * _from_row_tiles(ybuf_ref.at[0])
           + rec[:, R_GATE1:R_GATE1 + 1] * _from_row_tiles(ybuf_ref.at[1]))
    if final:
        out = _rms(out, gfin_ref[...], NORM_EPS)
    o_ref[...] = out


def _combine(dest_flat, x, rec, g_final, yb, final):
    n, d = x.shape
    tm = TOKEN_TILE
    return pl.pallas_call(
        functools.partial(_combine_kernel, tm=tm, final=final),
        grid=(n // tm,),
        in_specs=[pl.BlockSpec((2 * tm,), lambda i: (i,), memory_space=pltpu.SMEM),
                  pl.BlockSpec((tm, d), lambda i: (i, 0)), pl.BlockSpec((tm, LANES), lambda i: (i, 0)),
                  pl.BlockSpec(g_final.shape, lambda i: (0, 0)), pl.BlockSpec(memory_space=pl.ANY)],
        out_specs=pl.BlockSpec((tm, d), lambda i: (i, 0)),
        out_shape=jax.ShapeDtypeStruct((n, d), F32),
        scratch_shapes=[pltpu.VMEM((2, tm * ROW_TILE, LANES), F32), pltpu.SemaphoreType.DMA(())],
        compiler_params=_params(1), name="moe_combine",
    )(dest_flat, x, rec, g_final, yb)


def _moe(x, g, w_router, tri, w1, w3, w2, layer, g_final, final):
    n, d = x.shape
    rb = EXPERT_ROWS
    rec, counts = _router(x, g, w_router, tri)
    counts = counts[0, N_GROUPS:N_GROUPS + N_EXPERTS].astype(jnp.int32)
    padded = (counts + rb - 1) // rb * rb
    pad_end = jnp.cumsum(padded)
    pad_start = pad_end - padded
    n_blocks = (2 * n + N_EXPERTS * (rb - 1)) // rb
    expert = rec[:, R_E0:R_E1 + 1].astype(jnp.int32)
    rank = rec[:, R_RANK0:R_RANK1 + 1].astype(jnp.int32)
    dest = (pad_start[expert] + rank).reshape(-1)
    block_start = jnp.arange(n_blocks, dtype=jnp.int32) * rb
    block_expert = jnp.minimum(jnp.searchsorted(pad_end, block_start, side="right"), N_EXPERTS - 1).astype(jnp.int32)
    n_valid = (pad_end[-1:] // rb).astype(jnp.int32)
    xs = _dispatch(dest, x, g, jnp.zeros((n_blocks * rb * ROW_TILE, LANES), F32))
    yb = _experts(block_expert, n_valid, xs, w1, w3, w2, layer)
    return _combine(dest, x, rec, g_final, yb, final)


def _rope_tables(positions):
    inv = 1.0 / (ROPE_THETA ** (jnp.arange(0, MLA_ROPE, 2, dtype=F32) / MLA_ROPE))
    ang = positions.astype(F32)[:, None] * inv[None, :]
    cos = jnp.repeat(jnp.cos(ang), 2, axis=1)
    sin = jnp.stack([-jnp.sin(ang), jnp.sin(ang)], axis=-1).reshape(ang.shape[0], MLA_ROPE)
    pad = ((0, 0), (0, LANES - MLA_ROPE))
    return jnp.pad(cos, pad), jnp.pad(sin, pad)


def _pair_swap(w):
    return w.reshape(w.shape[0], -1, 2)[:, :, ::-1].reshape(w.shape)


def kernel(x_prompt, x_sample, ln_mix, ln_ffn, ln_final, da_wqkv, da_wo, da_lambda_q1, da_lambda_k1, da_lambda_q2, da_lambda_k2, da_subln, mla_w_down, mla_q_norm, mla_w_uq, mla_kv_norm, mla_w_ukv, mla_wo, moe_w_group, moe_w_expert, moe_w1, moe_w3, moe_w2):
    bp, sp, d = x_prompt.shape
    bs, ss, _ = x_sample.shape
    n_p, n_s = bp * sp, bs * ss
    groups = ((bp, sp, 0), (bs, ss, n_p))
    x = jnp.concatenate([x_prompt.reshape(n_p, d), x_sample.reshape(n_s, d)], axis=0)
    depth = ln_mix.shape[0]
    cdt = MXU_DTYPE

    positions = jnp.concatenate([jnp.tile(jnp.arange(sp), bp), jnp.tile(jnp.arange(ss), bs)])
    cos, sin = _rope_tables(positions)
    tri = jnp.tril(jnp.ones((TOKEN_TILE, TOKEN_TILE), F32), -1).astype(cdt)
    n_da_heads = d // (2 * DA_HEAD_DIM)
    slopes = 2.0 ** (-8.0 * np.arange(1, n_da_heads + 1) / n_da_heads) * LOG2E
    hd = MLA_HEADS * LANES
    zpad = LANES - MLA_ROPE

    for i in range(depth):
        j = i // 2
        g_mix = ln_mix[i][None, :]
        if i % 2 == 0:
            lambda_init = 0.8 - 0.6 * math.exp(-0.3 * i)
            lam = (jnp.exp(jnp.sum(da_lambda_q1[j] * da_lambda_k1[j])) - jnp.exp(jnp.sum(da_lambda_q2[j] * da_lambda_k2[j]))
                   + lambda_init)
            scalars = jnp.concatenate([jnp.asarray(slopes, F32), lam[None].astype(F32)])
            w = da_wqkv[j]
            q, k, vt = _da_qkv(x, g_mix, w[:, :d].astype(cdt), w[:, d:2 * d].astype(cdt), w[:, 2 * d:].T.astype(cdt),
                               DA_HEAD_DIM ** -0.5 * LOG2E)
            oa, ob = [_da_attn(scalars, q, k, vt, da_subln[j][:, None], batch, seq, row0, 1.0 - lambda_init)
                      for batch, seq, row0 in groups]
            x = _out_proj(x, oa, ob, da_wo[j].astype(cdt))
        else:
            qr, kvr = MLA_Q_RANK, MLA_KV_RANK
            wd = mla_w_down[j]
            w_rope = wd[:, qr + kvr:]
            wd = jnp.concatenate([wd[:, :qr + kvr], jnp.pad(w_rope, ((0, 0), (0, zpad))),
                                  jnp.pad(_pair_swap(w_rope), ((0, 0), (0, zpad)))], axis=1)
            wuq = mla_w_uq[j].reshape(qr, MLA_HEADS, MLA_NOPE + MLA_ROPE)
            wq_rope = wuq[:, :, MLA_NOPE:]
            pad3 = ((0, 0), (0, 0), (0, zpad))
            wuq = jnp.concatenate([wuq[:, :, :MLA_NOPE].reshape(qr, hd), jnp.pad(wq_rope, pad3).reshape(qr, hd),
                                   jnp.pad(_pair_swap(wq_rope.reshape(qr, -1)).reshape(wq_rope.shape), pad3).reshape(qr, hd)],
                                  axis=1)
            wukv = mla_w_ukv[j].reshape(kvr, MLA_HEADS, MLA_NOPE + MLA_V)
            wkn = wukv[:, :, :MLA_NOPE].reshape(kvr, hd)
            wvt = wukv[:, :, MLA_NOPE:].reshape(kvr, MLA_HEADS * MLA_V).T
            q, kn, kr, vt = _mla_proj(x, g_mix, wd.astype(cdt), mla_q_norm[j][None, :], mla_kv_norm[j][None, :],
                                      wuq.astype(cdt), wkn.astype(cdt), wvt.astype(cdt), cos, sin,
                                      (MLA_NOPE + MLA_ROPE) ** -0.5 * LOG2E)
            oa, ob = [_mla_attn(q, kn, kr, vt, batch, seq, row0) for batch, seq, row0 in groups]
            x = _out_proj(x, oa, ob, mla_wo[j].astype(cdt))
        w_router = jnp.pad(jnp.concatenate([moe_w_group[i], moe_w_expert[i]], axis=1),
                           ((0, 0), (0, LANES - N_GROUPS - N_EXPERTS)))
        x = _moe(x, ln_ffn[i][None, :], w_router, tri, moe_w1, moe_w3, moe_w2, i, ln_final[None, :],
                 final=(i == depth - 1))
    return x[:n_p].reshape(bp, sp, d), x[n_p:].reshape(bs, ss, d)
```

```python
import functools
import math

import numpy as np
import jax
import jax.numpy as jnp
from jax import lax
from jax.experimental import pallas as pl
from jax.experimental.pallas import tpu as pltpu

F32 = jnp.float32
MXU_DTYPE = jnp.bfloat16
LOG2E = 1.4426950408889634

NORM_EPS = 1e-6
DA_SUBLN_EPS = 1e-5
ROPE_THETA = 10000.0
DA_HEAD_DIM = 64
MLA_HEADS = 8
MLA_NOPE = 128
MLA_ROPE = 64
MLA_V = 128
MLA_Q_RANK = 384
MLA_KV_RANK = 256
N_GROUPS = 8
EXPERTS_PER_GROUP = 8
N_EXPERTS = N_GROUPS * EXPERTS_PER_GROUP

LANES = 128
TOKEN_TILE = 512
KV_CHUNK = 512
DA_Q_TILE = 256
MLA_Q_TILE = 512
MLA_COLS = 256
EXPERT_ROWS = 256
VMEM_LIMIT = 56 * 1024 * 1024


def _rms(x, g, eps):
    return x * lax.rsqrt(jnp.mean(x * x, axis=-1, keepdims=True) + eps) * g


def _nt_dot(a, b):
    return lax.dot_general(a, b, (((1,), (1,)), ((), ())), preferred_element_type=F32)


def _params(n_axes):
    return pltpu.CompilerParams(dimension_semantics=("arbitrary",) * n_axes, vmem_limit_bytes=VMEM_LIMIT)


def _da_qkv_kernel(x_ref, g_ref, wq_ref, wk_ref, wvt_ref, q_ref, k_ref, vt_ref, *, q_scale):
    xn = _rms(x_ref[...], g_ref[...], NORM_EPS).astype(MXU_DTYPE)
    q_ref[...] = (jnp.dot(xn, wq_ref[...], preferred_element_type=F32) * q_scale).astype(q_ref.dtype)
    k_ref[...] = jnp.dot(xn, wk_ref[...], preferred_element_type=F32).astype(k_ref.dtype)
    vt_ref[0] = _nt_dot(wvt_ref[...], xn).astype(vt_ref.dtype)


def _da_qkv(x, g, wq, wk, wvt, q_scale):
    n, d = x.shape
    tm = TOKEN_TILE
    full = lambda a: pl.BlockSpec(a.shape, lambda i: (0,) * a.ndim)
    return pl.pallas_call(
        functools.partial(_da_qkv_kernel, q_scale=q_scale),
        grid=(n // tm,),
        in_specs=[pl.BlockSpec((tm, d), lambda i: (i, 0)), full(g), full(wq), full(wk), full(wvt)],
        out_specs=[pl.BlockSpec((tm, d), lambda i: (i, 0)), pl.BlockSpec((tm, d), lambda i: (i, 0)),
                   pl.BlockSpec((1, d, tm), lambda i: (i, 0, 0))],
        out_shape=[jax.ShapeDtypeStruct((n, d), MXU_DTYPE), jax.ShapeDtypeStruct((n, d), MXU_DTYPE),
                   jax.ShapeDtypeStruct((n // tm, d, tm), MXU_DTYPE)],
        compiler_params=_params(1), name="da_qkv",
    )(x, g, wq, wk, wvt)


def _mla_proj_kernel(x_ref, g_ref, wd_ref, qg_ref, kvg_ref, wuq_ref, wkn_ref, wvt_ref, cos_ref, sin_ref,
                     q_ref, kn_ref, kr_ref, vt_ref, *, q_scale):
    xn = _rms(x_ref[...], g_ref[...], NORM_EPS).astype(MXU_DTYPE)
    down = jnp.dot(xn, wd_ref[...], preferred_element_type=F32)
    qr, kvr = MLA_Q_RANK, MLA_KV_RANK
    c_q = _rms(down[:, :qr], qg_ref[...], NORM_EPS).astype(MXU_DTYPE)
    c_kv = _rms(down[:, qr:qr + kvr], kvg_ref[...], NORM_EPS).astype(MXU_DTYPE)
    cos, sin = cos_ref[...], sin_ref[...]
    kr = down[:, qr + kvr:qr + kvr + LANES] * cos + down[:, qr + kvr + LANES:] * sin
    kr_ref[...] = kr.astype(kr_ref.dtype)
    hd = MLA_HEADS * LANES
    q = jnp.dot(c_q, wuq_ref[...], preferred_element_type=F32)
    q_ref[:, :hd] = (q[:, :hd] * q_scale).astype(q_ref.dtype)
    for h in range(MLA_HEADS):
        a = q[:, hd + h * LANES:hd + (h + 1) * LANES]
        b = q[:, 2 * hd + h * LANES:2 * hd + (h + 1) * LANES]
        q_ref[:, hd + h * LANES:hd + (h + 1) * LANES] = ((a * cos + b * sin) * q_scale).astype(q_ref.dtype)
    kn_ref[...] = jnp.dot(c_kv, wkn_ref[...], preferred_element_type=F32).astype(kn_ref.dtype)
    vt_ref[0] = _nt_dot(wvt_ref[...], c_kv).astype(vt_ref.dtype)


def _mla_proj(x, g, wd, qg, kvg, wuq, wkn, wvt, cos, sin, q_scale):
    n, d = x.shape
    tm = TOKEN_TILE
    hd = MLA_HEADS * LANES
    full = lambda a: pl.BlockSpec(a.shape, lambda i: (0,) * a.ndim)
    row = lambda w: pl.BlockSpec((tm, w), lambda i: (i, 0))
    return pl.pallas_call(
        functools.partial(_mla_proj_kernel, q_scale=q_scale),
        grid=(n // tm,),
        in_specs=[row(d), full(g), full(wd), full(qg), full(kvg), full(wuq), full(wkn), full(wvt),
                  row(LANES), row(LANES)],
        out_specs=[row(2 * hd), row(hd), row(LANES), pl.BlockSpec((1, hd, tm), lambda i: (i, 0, 0))],
        out_shape=[jax.ShapeDtypeStruct((n, 2 * hd), MXU_DTYPE), jax.ShapeDtypeStruct((n, hd), MXU_DTYPE),
                   jax.ShapeDtypeStruct((n, LANES), MXU_DTYPE),
                   jax.ShapeDtypeStruct((n // tm, hd, tm), MXU_DTYPE)],
        compiler_params=_params(1), name="mla_proj",
    )(x, g, wd, qg, kvg, wuq, wkn, wvt, cos, sin)


def _out_proj_kernel(x_ref, oa_ref, ob_ref, w_ref, y_ref, *, a_tiles):
    @pl.when(pl.program_id(0) < a_tiles)
    def _():
        y_ref[...] = x_ref[...] + jnp.dot(oa_ref[...], w_ref[...], preferred_element_type=F32)

    @pl.when(pl.program_id(0) >= a_tiles)
    def _():
        y_ref[...] = x_ref[...] + jnp.dot(ob_ref[...], w_ref[...], preferred_element_type=F32)


def _out_proj(x, oa, ob, w):
    n, d = x.shape
    tm = TOKEN_TILE
    a_tiles = oa.shape[0] // tm
    assert oa.shape[0] % tm == 0 and ob.shape[0] % tm == 0 and oa.shape[0] + ob.shape[0] == n
    return pl.pallas_call(
        functools.partial(_out_proj_kernel, a_tiles=a_tiles),
        grid=(n // tm,),
        in_specs=[pl.BlockSpec((tm, d), lambda i: (i, 0)),
                  pl.BlockSpec((tm, oa.shape[1]), lambda i: (jnp.minimum(i, a_tiles - 1), 0)),
                  pl.BlockSpec((tm, ob.shape[1]), lambda i: (jnp.maximum(i - a_tiles, 0), 0)),
                  pl.BlockSpec(w.shape, lambda i: (0, 0))],
        out_specs=pl.BlockSpec((tm, d), lambda i: (i, 0)),
        out_shape=jax.ShapeDtypeStruct((n, d), F32),
        compiler_params=_params(1), name="out_proj",
    )(x, oa, ob, w)


FLASH_STRIP = 32
FLASH_ROUNDS = 2
FLASH_SLOTS = 3


def _flash_scratch(tk, w, dv):
    n = FLASH_SLOTS
    return ([pltpu.VMEM((tk, w), F32)] * n + [pltpu.VMEM((tk, w), MXU_DTYPE)] * n + [pltpu.VMEM((1, w), F32)] * (2 * n)
            + [pltpu.VMEM((1, w), F32), pltpu.VMEM((1, w), F32), pltpu.VMEM((dv, w), F32)])


def _flash_t(q, k_ref, vt_ref, n_chunks, tk, scratch, bias_fn, cols):
    n = FLASH_SLOTS
    s_bufs, p_bufs, c_bufs, a_bufs = (scratch[i * n:(i + 1) * n] for i in range(4))
    m_ref, l_ref, acc_ref = scratch[4 * n:4 * n + 3]
    assert n_chunks >= 2
    m_ref[...] = jnp.full(m_ref.shape, -jnp.inf, F32)
    l_ref[...] = jnp.zeros(l_ref.shape, F32)
    acc_ref[...] = jnp.zeros(acc_ref.shape, F32)

    w = q.shape[0]
    groups = [slice(g, g + cols) for g in range(0, w, cols)]

    def scores_g(c, slot, g):
        k = k_ref[pl.ds(pl.multiple_of(c * tk, tk), tk), :]
        st = _nt_dot(k, q[g, :])
        if bias_fn is not None:
            st = bias_fn(st, c)
        cmax = None
        for r in range(0, tk, FLASH_STRIP):
            strip = st[r:r + FLASH_STRIP, :]
            s_bufs[slot][r:r + FLASH_STRIP, g] = strip
            cmax = strip if cmax is None else jnp.maximum(cmax, strip)
        c_bufs[slot][:, g] = jnp.max(cmax, axis=0, keepdims=True)

    def softmax_g(slot, g):
        m_prev = m_ref[:, g]
        m_new = jnp.maximum(m_prev, c_bufs[slot][:, g])
        alpha = jnp.exp2(m_prev - m_new)
        m_ref[:, g] = m_new
        a_bufs[slot][:, g] = alpha
        psum = None
        for r in range(0, tk, FLASH_STRIP):
            p = jnp.exp2(s_bufs[slot][r:r + FLASH_STRIP, g] - m_new)
            psum = p if psum is None else psum + p
            p_bufs[slot][r:r + FLASH_STRIP, g] = p.astype(MXU_DTYPE)
        l_ref[:, g] = alpha * l_ref[:, g] + jnp.sum(psum, axis=0, keepdims=True)

    def values_g(c, slot, g):
        tv = vt_ref.shape[2]
        pv = None
        for j in range(tk // tv):
            part = jnp.dot(vt_ref[c * (tk // tv) + j], p_bufs[slot][j * tv:(j + 1) * tv, g],
                           preferred_element_type=F32)
            pv = part if pv is None else pv + part
        acc_ref[:, g] = acc_ref[:, g] * a_bufs[slot][:, g] + pv

    def scores(c, slot):
        for g in groups:
            scores_g(c, slot, g)

    def softmax(slot):
        for g in groups:
            softmax_g(slot, g)

    def values(c, slot):
        for g in groups:
            values_g(c, slot, g)

    def step(c, slot):
        for g in groups:
            values_g(c - 1, (slot - 1) % n, g)
            scores_g(c + 1, (slot + 1) % n, g)
            softmax_g(slot, g)

    scores(0, 0)
    scores(1, 1)
    softmax(0)
    steady = n_chunks - 2
    per_trip = n * FLASH_ROUNDS
    trips = steady // per_trip

    def body(j, carry):
        for r in range(per_trip):
            step(1 + per_trip * j + r, (1 + r) % n)
        return carry

    lax.fori_loop(0, trips, body, 0)
    for c in range(1 + per_trip * trips, n_chunks - 1):
        step(c, c % n)
    values(n_chunks - 2, (n_chunks - 2) % n)
    softmax((n_chunks - 1) % n)
    values(n_chunks - 1, (n_chunks - 1) % n)


def _kv_chunk(seq):
    return min(KV_CHUNK, max(TOKEN_TILE, seq // 2))


def _da_attn_kernel(sc_ref, q_ref, k_ref, vt_ref, g_ref, o_ref, dist_ref, *scratch,
                    tq, tk, n_chunks, n_heads, out_scale):
    l_ref, acc_ref = scratch[4 * FLASH_SLOTS + 1:]
    h, qi = pl.program_id(1), pl.program_id(2)
    slope, lam = sc_ref[h], sc_ref[n_heads]
    q = q_ref[...]
    lane = lax.broadcasted_iota(jnp.int32, q.shape, 1)
    zero = jnp.zeros_like(q)
    d = q.shape[1] // 2
    qs = jnp.concatenate([jnp.where(lane < d, q, zero), jnp.where(lane >= d, q, zero)], axis=0)
    row = lax.broadcasted_iota(jnp.int32, (tk, tq), 0)
    col = lax.broadcasted_iota(jnp.int32, (tk, tq), 1)
    dist_ref[...] = (row - col - qi * tq).astype(F32)

    def bias_fn(st, c):
        b = slope * jnp.abs(dist_ref[...] + lax.convert_element_type(c * tk, F32))
        return jnp.concatenate([st[:, :tq] - b, st[:, tq:] - b], axis=1)

    _flash_t(qs, k_ref, vt_ref, n_chunks, tk, scratch, bias_fn, 2 * tq)
    o = acc_ref[...] * (1.0 / l_ref[...])
    o = o[:, :tq] - lam * o[:, tq:]
    o = o * lax.rsqrt(jnp.mean(o * o, axis=0, keepdims=True) + DA_SUBLN_EPS) * g_ref[...] * out_scale
    o_ref[...] = o.T.astype(o_ref.dtype)


def _da_attn(scalars, q, k, vt, subln_col, batch, seq, row0, out_scale):
    dm = q.shape[1]
    tq, tk, tv = DA_Q_TILE, _kv_chunk(seq), vt.shape[2]
    hw = 2 * DA_HEAD_DIM
    n_heads = dm // hw
    assert row0 % seq == 0 and seq % tq == 0 and seq % tk == 0 and tk % tv == 0
    s0, q0, nq, n_chunks = row0 // seq, row0 // tq, seq // tq, seq // tk
    return pl.pallas_call(
        functools.partial(_da_attn_kernel, tq=tq, tk=tk, n_chunks=n_chunks, n_heads=n_heads, out_scale=out_scale),
        grid_spec=pltpu.PrefetchScalarGridSpec(
            num_scalar_prefetch=1, grid=(batch, n_heads, nq),
            in_specs=[pl.BlockSpec((tq, hw), lambda b, h, i, sc: (q0 + b * nq + i, h)),
                      pl.BlockSpec((seq, hw), lambda b, h, i, sc: (s0 + b, h)),
                      pl.BlockSpec((seq // tv, hw, tv), lambda b, h, i, sc: (s0 + b, h, 0)),
                      pl.BlockSpec(subln_col.shape, lambda b, h, i, sc: (0, 0))],
            out_specs=pl.BlockSpec((tq, hw), lambda b, h, i, sc: (b * nq + i, h)),
            scratch_shapes=[pltpu.VMEM((tk, tq), F32)] + _flash_scratch(tk, 2 * tq, hw)),
        out_shape=jax.ShapeDtypeStruct((batch * seq, dm), MXU_DTYPE),
        compiler_params=_params(3), name="da_attn",
    )(scalars, q, k, vt, subln_col)


def _mla_attn_kernel(qn_ref, qr_ref, kn_ref, kr_ref, vt_ref, o_ref, kfull_ref, *scratch, tk, n_chunks):
    l_ref, acc_ref = scratch[4 * FLASH_SLOTS + 1:]

    @pl.when(pl.program_id(2) == 0)
    def _():
        def copy(c, carry):
            rows = pl.ds(pl.multiple_of(c * tk, tk), tk)
            kfull_ref[rows, :LANES] = kn_ref[rows, :]
            kfull_ref[rows, LANES:] = kr_ref[rows, :]
            return carry
        lax.fori_loop(0, n_chunks, copy, 0)

    q = jnp.concatenate([qn_ref[...], qr_ref[...]], axis=1)
    _flash_t(q, kfull_ref, vt_ref, n_chunks, tk, scratch, None, MLA_COLS)
    o = acc_ref[...] * (1.0 / l_ref[...])
    o_ref[...] = o.T.astype(o_ref.dtype)


def _mla_attn(q, kn, kr, vt, batch, seq, row0):
    n_heads = MLA_HEADS
    tq, tk, tv = MLA_Q_TILE, _kv_chunk(seq), vt.shape[2]
    assert row0 % seq == 0 and seq % tq == 0 and seq % tk == 0 and tk % tv == 0
    s0, q0, nq, n_chunks = row0 // seq, row0 // tq, seq // tq, seq // tk
    return pl.pallas_call(
        functools.partial(_mla_attn_kernel, tk=tk, n_chunks=n_chunks),
        grid=(batch, n_heads, nq),
        in_specs=[pl.BlockSpec((tq, LANES), lambda b, h, i: (q0 + b * nq + i, h)),
                  pl.BlockSpec((tq, LANES), lambda b, h, i: (q0 + b * nq + i, n_heads + h)),
                  pl.BlockSpec((seq, LANES), lambda b, h, i: (s0 + b, h)),
                  pl.BlockSpec((seq, LANES), lambda b, h, i: (s0 + b, 0)),
                  pl.BlockSpec((seq // tv, MLA_V, tv), lambda b, h, i: (s0 + b, h, 0))],
        out_specs=pl.BlockSpec((tq, MLA_V), lambda b, h, i: (b * nq + i, h)),
        out_shape=jax.ShapeDtypeStruct((batch * seq, n_heads * MLA_V), MXU_DTYPE),
        scratch_shapes=[pltpu.VMEM((seq, 2 * LANES), MXU_DTYPE)] + _flash_scratch(tk, tq, MLA_V),
        compiler_params=_params(3), name="mla_attn",
    )(q, q, kn, kr, vt)


R_E0, R_E1, R_RANK0, R_RANK1, R_GATE0, R_GATE1 = range(6)


def _router_kernel(x_ref, g_ref, w_ref, tri_ref, rec_ref, cnt_ref, base_ref):
    i = pl.program_id(0)

    @pl.when(i == 0)
    def _():
        base_ref[...] = jnp.zeros(base_ref.shape, F32)

    xn = _rms(x_ref[...], g_ref[...], NORM_EPS)
    logits = jnp.dot(xn, w_ref[...], preferred_element_type=F32, precision=lax.Precision.HIGHEST)
    tm = logits.shape[0]
    lane = lax.broadcasted_iota(jnp.int32, logits.shape, 1)
    neg = jnp.float32(-jnp.inf)

    def first_lane(mask):
        return jnp.min(jnp.where(mask, lane, LANES), axis=1, keepdims=True)

    is_g = lane < N_GROUPS
    lg = jnp.where(is_g, logits, neg)
    mg = jnp.max(lg, axis=1, keepdims=True)
    pg_sel = 1.0 / jnp.sum(jnp.exp(lg - mg), axis=1, keepdims=True)
    g_sel = first_lane(lg == mg)
    e_lo = N_GROUPS + EXPERTS_PER_GROUP * g_sel
    in_e = (lane >= e_lo) & (lane < e_lo + EXPERTS_PER_GROUP)
    le = jnp.where(in_e, logits, neg)
    me = jnp.max(le, axis=1, keepdims=True)
    ee = jnp.exp(le - me)
    pe = ee / jnp.sum(ee, axis=1, keepdims=True)
    pe = jnp.where(in_e, pe, -1.0)
    p0 = jnp.max(pe, axis=1, keepdims=True)
    i0 = first_lane(pe == p0)
    pe1 = jnp.where(lane == i0, -1.0, pe)
    p1 = jnp.max(pe1, axis=1, keepdims=True)
    i1 = first_lane(pe1 == p1)
    gate0 = pg_sel * p0 / (p0 + p1)
    gate1 = pg_sel * p1 / (p0 + p1)
    hit0, hit1 = lane == i0, lane == i1
    oh = jnp.concatenate([jnp.where(hit0, 1.0, 0.0), jnp.where(hit1, 1.0, 0.0)], axis=1)
    before = jnp.dot(tri_ref[...], oh.astype(MXU_DTYPE), preferred_element_type=F32)
    base = base_ref[...]
    tot0 = jnp.sum(oh[:, :LANES], axis=0, keepdims=True)
    tot1 = jnp.sum(oh[:, LANES:], axis=0, keepdims=True)
    rank0 = jnp.sum(jnp.where(hit0, before[:, :LANES] + base, 0.0), axis=1, keepdims=True)
    rank1 = jnp.sum(jnp.where(hit1, before[:, LANES:] + (base + tot0), 0.0), axis=1, keepdims=True)
    base = base + tot0 + tot1
    base_ref[...] = base
    cnt_ref[...] = base
    rec = jnp.zeros(logits.shape, F32)
    for pos, val in ((R_E0, (i0 - N_GROUPS).astype(F32)), (R_E1, (i1 - N_GROUPS).astype(F32)),
                     (R_RANK0, rank0), (R_RANK1, rank1), (R_GATE0, gate0), (R_GATE1, gate1)):
        rec = jnp.where(lane == pos, val, rec)
    rec_ref[...] = rec


def _router(x, g, w_router, tri):
    n, d = x.shape
    tm = TOKEN_TILE
    full = lambda a: pl.BlockSpec(a.shape, lambda i: (0,) * a.ndim)
    return pl.pallas_call(
        _router_kernel,
        grid=(n // tm,),
        in_specs=[pl.BlockSpec((tm, d), lambda i: (i, 0)), full(g), full(w_router), full(tri)],
        out_specs=[pl.BlockSpec((tm, LANES), lambda i: (i, 0)), pl.BlockSpec((1, LANES), lambda i: (0, 0))],
        out_shape=[jax.ShapeDtypeStruct((n, LANES), F32), jax.ShapeDtypeStruct((1, LANES), F32)],
        scratch_shapes=[pltpu.VMEM((1, LANES), F32)],
        compiler_params=_params(1), name="moe_router",
    )(x, g, w_router, tri)


ROW_TILE = 8


def _row_copy(src_ref, src_row, dst_ref, dst_row, sem):
    src = src_ref.at[pl.ds(pl.multiple_of(src_row * ROW_TILE, ROW_TILE), ROW_TILE), :]
    dst = dst_ref.at[pl.ds(pl.multiple_of(dst_row * ROW_TILE, ROW_TILE), ROW_TILE), :]
    return pltpu.make_async_copy(src, dst, sem)


def _to_row_tiles(ref, x):
    rows = x.shape[0]
    for c in range(ROW_TILE):
        ref[pl.ds(c, rows, stride=ROW_TILE), :] = x[:, c * LANES:(c + 1) * LANES]


def _from_row_tiles(ref):
    rows = ref.shape[0] // ROW_TILE
    return jnp.concatenate([ref[pl.ds(c, rows, stride=ROW_TILE), :] for c in range(ROW_TILE)], axis=1)


def _dispatch_kernel(dest_ref, x_ref, g_ref, xs_in_ref, xs_ref, xn_ref, sem, *, tm):
    del xs_in_ref
    _to_row_tiles(xn_ref, _rms(x_ref[...], g_ref[...], NORM_EPS))

    def issue(r, carry):
        for c in range(2):
            _row_copy(xn_ref, r, xs_ref, dest_ref[2 * r + c], sem).start()
        return carry
    lax.fori_loop(0, tm, issue, 0, unroll=8)

    def drain(r, carry):
        for c in range(2):
            _row_copy(xn_ref, 0, xs_ref, 0, sem).wait()
        return carry
    lax.fori_loop(0, tm, drain, 0, unroll=8)


def _dispatch(dest_flat, x, g, xs_zero):
    n, d = x.shape
    tm = TOKEN_TILE
    return pl.pallas_call(
        functools.partial(_dispatch_kernel, tm=tm),
        grid=(n // tm,),
        in_specs=[pl.BlockSpec((2 * tm,), lambda i: (i,), memory_space=pltpu.SMEM),
                  pl.BlockSpec((tm, d), lambda i: (i, 0)), pl.BlockSpec(g.shape, lambda i: (0, 0)),
                  pl.BlockSpec(memory_space=pl.ANY)],
        out_specs=pl.BlockSpec(memory_space=pl.ANY),
        out_shape=jax.ShapeDtypeStruct(xs_zero.shape, F32),
        scratch_shapes=[pltpu.VMEM((tm * ROW_TILE, LANES), F32), pltpu.SemaphoreType.DMA(())],
        input_output_aliases={3: 0},
        compiler_params=_params(1), name="moe_dispatch",
    )(dest_flat, x, g, xs_zero)


def _expert_kernel(be_ref, nv_ref, xs_ref, w1_ref, w3_ref, w2_ref, y_ref, w1c_ref, w3c_ref, w2c_ref):
    b = pl.program_id(0)

    @pl.when((b == 0) | (be_ref[b] != be_ref[jnp.maximum(b - 1, 0)]))
    def _():
        w1c_ref[...] = w1_ref[0, 0].astype(MXU_DTYPE)
        w3c_ref[...] = w3_ref[0, 0].astype(MXU_DTYPE)
        w2c_ref[...] = w2_ref[0, 0].astype(MXU_DTYPE)

    @pl.when(b < nv_ref[0])
    def _():
        xb = _from_row_tiles(xs_ref).astype(MXU_DTYPE)
        h1 = jnp.dot(xb, w1c_ref[...], preferred_element_type=F32)
        h3 = jnp.dot(xb, w3c_ref[...], preferred_element_type=F32)
        h = (h1 * jax.nn.sigmoid(h1) * h3).astype(MXU_DTYPE)
        _to_row_tiles(y_ref, jnp.dot(h, w2c_ref[...], preferred_element_type=F32))

    @pl.when(b >= nv_ref[0])
    def _():
        y_ref[...] = jnp.zeros(y_ref.shape, F32)


def _experts(block_expert, n_valid, xs, w1, w3, w2, layer):
    rb = EXPERT_ROWS
    d, de = w1.shape[2:]
    assert d == ROW_TILE * LANES
    return pl.pallas_call(
        _expert_kernel,
        grid_spec=pltpu.PrefetchScalarGridSpec(
            num_scalar_prefetch=2, grid=(xs.shape[0] // (rb * ROW_TILE),),
            in_specs=[pl.BlockSpec((rb * ROW_TILE, LANES), lambda b, be, nv: (b, 0)),
                      pl.BlockSpec((1, 1, d, de), lambda b, be, nv: (layer, be[b], 0, 0)),
                      pl.BlockSpec((1, 1, d, de), lambda b, be, nv: (layer, be[b], 0, 0)),
                      pl.BlockSpec((1, 1, de, d), lambda b, be, nv: (layer, be[b], 0, 0))],
            out_specs=pl.BlockSpec((rb * ROW_TILE, LANES), lambda b, be, nv: (b, 0)),
            scratch_shapes=[pltpu.VMEM((d, de), MXU_DTYPE), pltpu.VMEM((d, de), MXU_DTYPE),
                            pltpu.VMEM((de, d), MXU_DTYPE)]),
        out_shape=jax.ShapeDtypeStruct(xs.shape, F32),
        compiler_params=_params(1), name="moe_experts",
    )(block_expert, n_valid, xs, w1, w3, w2)


def _combine_kernel(dest_ref, x_ref, rec_ref, gfin_ref, yb_ref, o_ref, ybuf_ref, sem, *, tm, final):
    def issue(r, carry):
        for c in range(2):
            _row_copy(yb_ref, dest_ref[2 * r + c], ybuf_ref.at[c], r, sem).start()
        return carry
    lax.fori_loop(0, tm, issue, 0, unroll=8)

    def drain(r, carry):
        for c in range(2):
            _row_copy(yb_ref, 0, ybuf_ref.at[c], 0, sem).wait()
        return carry
    lax.fori_loop(0, tm, drain, 0, unroll=8)

    rec = rec_ref[...]
    out = (x_ref[...] + rec[:, R_GATE0:R_GATE0 + 1] * _from_row_tiles(ybuf_ref.at[0])
           + rec[:, R_GATE1:R_GATE1 + 1] * _from_row_tiles(ybuf_ref.at[1]))
    if final:
        out = _rms(out, gfin_ref[...], NORM_EPS)
    o_ref[...] = out


def _combine(dest_flat, x, rec, g_final, yb, final):
    n, d = x.shape
    tm = TOKEN_TILE
    return pl.pallas_call(
        functools.partial(_combine_kernel, tm=tm, final=final),
        grid=(n // tm,),
        in_specs=[pl.BlockSpec((2 * tm,), lambda i: (i,), memory_space=pltpu.SMEM),
                  pl.BlockSpec((tm, d), lambda i: (i, 0)), pl.BlockSpec((tm, LANES), lambda i: (i, 0)),
                  pl.BlockSpec(g_final.shape, lambda i: (0, 0)), pl.BlockSpec(memory_space=pl.ANY)],
        out_specs=pl.BlockSpec((tm, d), lambda i: (i, 0)),
        out_shape=jax.ShapeDtypeStruct((n, d), F32),
        scratch_shapes=[pltpu.VMEM((2, tm * ROW_TILE, LANES), F32), pltpu.SemaphoreType.DMA(())],
        compiler_params=_params(1), name="moe_combine",
    )(dest_flat, x, rec, g_final, yb)


def _moe(x, g, w_router, tri, w1, w3, w2, layer, g_final, final):
    n, d = x.shape
    rb = EXPERT_ROWS
    rec, counts = _router(x, g, w_router, tri)
    counts = counts[0, N_GROUPS:N_GROUPS + N_EXPERTS].astype(jnp.int32)
    padded = (counts + rb - 1) // rb * rb
    pad_end = jnp.cumsum(padded)
    pad_start = pad_end - padded
    n_blocks = (2 * n + N_EXPERTS * (rb - 1)) // rb
    expert = rec[:, R_E0:R_E1 + 1].astype(jnp.int32)
    rank = rec[:, R_RANK0:R_RANK1 + 1].astype(jnp.int32)
    before = expert[..., None] > jnp.arange(N_EXPERTS, dtype=jnp.int32)
    dest = (jnp.sum(jnp.where(before, padded, 0), axis=-1) + rank).reshape(-1)
    block_start = jnp.arange(n_blocks, dtype=jnp.int32) * rb
    block_expert = jnp.minimum(jnp.searchsorted(pad_end, block_start, side="right"), N_EXPERTS - 1).astype(jnp.int32)
    n_valid = (pad_end[-1:] // rb).astype(jnp.int32)
    xs = _dispatch(dest, x, g, jnp.zeros((n_blocks * rb * ROW_TILE, LANES), F32))
    yb = _experts(block_expert, n_valid, xs, w1, w3, w2, layer)
    return _combine(dest, x, rec, g_final, yb, final)


def _rope_tables(positions):
    inv = 1.0 / (ROPE_THETA ** (jnp.arange(0, MLA_ROPE, 2, dtype=F32) / MLA_ROPE))
    ang = positions.astype(F32)[:, None] * inv[None, :]
    cos = jnp.repeat(jnp.cos(ang), 2, axis=1)
    sin = jnp.stack([-jnp.sin(ang), jnp.sin(ang)], axis=-1).reshape(ang.shape[0], MLA_ROPE)
    pad = ((0, 0), (0, LANES - MLA_ROPE))
    return jnp.pad(cos, pad), jnp.pad(sin, pad)


def _pair_swap(w):
    return w.reshape(w.shape[0], -1, 2)[:, :, ::-1].reshape(w.shape)


def kernel(x_prompt, x_sample, ln_mix, ln_ffn, ln_final, da_wqkv, da_wo, da_lambda_q1, da_lambda_k1, da_lambda_q2, da_lambda_k2, da_subln, mla_w_down, mla_q_norm, mla_w_uq, mla_kv_norm, mla_w_ukv, mla_wo, moe_w_group, moe_w_expert, moe_w1, moe_w3, moe_w2):
    bp, sp, d = x_prompt.shape
    bs, ss, _ = x_sample.shape
    n_p, n_s = bp * sp, bs * ss
    groups = ((bp, sp, 0), (bs, ss, n_p))
    x = jnp.concatenate([x_prompt.reshape(n_p, d), x_sample.reshape(n_s, d)], axis=0)
    depth = ln_mix.shape[0]
    cdt = MXU_DTYPE

    positions = jnp.concatenate([jnp.tile(jnp.arange(sp), bp), jnp.tile(jnp.arange(ss), bs)])
    cos, sin = _rope_tables(positions)
    tri = jnp.tril(jnp.ones((TOKEN_TILE, TOKEN_TILE), F32), -1).astype(cdt)
    n_da_heads = d // (2 * DA_HEAD_DIM)
    slopes = 2.0 ** (-8.0 * np.arange(1, n_da_heads + 1) / n_da_heads) * LOG2E
    hd = MLA_HEADS * LANES
    zpad = LANES - MLA_ROPE

    for i in range(depth):
        j = i // 2
        g_mix = ln_mix[i][None, :]
        if i % 2 == 0:
            lambda_init = 0.8 - 0.6 * math.exp(-0.3 * i)
            lam = (jnp.exp(jnp.sum(da_lambda_q1[j] * da_lambda_k1[j])) - jnp.exp(jnp.sum(da_lambda_q2[j] * da_lambda_k2[j]))
                   + lambda_init)
            scalars = jnp.concatenate([jnp.asarray(slopes, F32), lam[None].astype(F32)])
            w = da_wqkv[j]
            q, k, vt = _da_qkv(x, g_mix, w[:, :d].astype(cdt), w[:, d:2 * d].astype(cdt), w[:, 2 * d:].T.astype(cdt),
                               DA_HEAD_DIM ** -0.5 * LOG2E)
            oa, ob = [_da_attn(scalars, q, k, vt, da_subln[j][:, None], batch, seq, row0, 1.0 - lambda_init)
                      for batch, seq, row0 in groups]
            x = _out_proj(x, oa, ob, da_wo[j].astype(cdt))
        else:
            qr, kvr = MLA_Q_RANK, MLA_KV_RANK
            wd = mla_w_down[j]
            w_rope = wd[:, qr + kvr:]
            wd = jnp.concatenate([wd[:, :qr + kvr], jnp.pad(w_rope, ((0, 0), (0, zpad))),
                                  jnp.pad(_pair_swap(w_rope), ((0, 0), (0, zpad)))], axis=1)
            wuq = mla_w_uq[j].reshape(qr, MLA_HEADS, MLA_NOPE + MLA_ROPE)
            wq_rope = wuq[:, :, MLA_NOPE:]
            pad3 = ((0, 0), (0, 0), (0, zpad))
            wuq = jnp.concatenate([wuq[:, :, :MLA_NOPE].reshape(qr, hd), jnp.pad(wq_rope, pad3).reshape(qr, hd),
                                   jnp.pad(_pair_swap(wq_rope.reshape(qr, -1)).reshape(wq_rope.shape), pad3).reshape(qr, hd)],
                                  axis=1)
            wukv = mla_w_ukv[j].reshape(kvr, MLA_HEADS, MLA_NOPE + MLA_V)
            wkn = wukv[:, :, :MLA_NOPE].reshape(kvr, hd)
            wvt = wukv[:, :, MLA_NOPE:].reshape(kvr, MLA_HEADS * MLA_V).T
            q, kn, kr, vt = _mla_proj(x, g_mix, wd.astype(cdt), mla_q_norm[j][None, :], mla_kv_norm[j][None, :],
                                      wuq.astype(cdt), wkn.astype(cdt), wvt.astype(cdt), cos, sin,
                                      (MLA_NOPE + MLA_ROPE) ** -0.5 * LOG2E)
            oa, ob = [_mla_attn(q, kn, kr, vt, batch, seq, row0) for batch, seq, row0 in groups]
            x = _out_proj(x, oa, ob, mla_wo[j].astype(cdt))
        w_router = jnp.pad(jnp.concatenate([moe_w_group[i], moe_w_expert[i]], axis=1),
                           ((0, 0), (0, LANES - N_GROUPS - N_EXPERTS)))
        x = _moe(x, ln_ffn[i][None, :], w_router, tri, moe_w1, moe_w3, moe_w2, i, ln_final[None, :],
                 final=(i == depth - 1))
    return x[:n_p].reshape(bp, sp, d), x[n_p:].reshape(bs, ss, d)
```

```python
import functools
import math

import numpy as np
import jax
import jax.numpy as jnp
from jax import lax
from jax.experimental import pallas as pl
from jax.experimental.pallas import tpu as pltpu

F32 = jnp.float32
MXU_DTYPE = jnp.bfloat16
LOG2E = 1.4426950408889634

NORM_EPS = 1e-6
DA_SUBLN_EPS = 1e-5
ROPE_THETA = 10000.0
DA_HEAD_DIM = 64
MLA_HEADS = 8
MLA_NOPE = 128
MLA_ROPE = 64
MLA_V = 128
MLA_Q_RANK = 384
MLA_KV_RANK = 256
N_GROUPS = 8
EXPERTS_PER_GROUP = 8
N_EXPERTS = N_GROUPS * EXPERTS_PER_GROUP

LANES = 128
TOKEN_TILE = 512
KV_CHUNK = 512
DA_Q_TILE = 256
MLA_Q_TILE = 512
MLA_COLS = 256
EXPERT_ROWS = 256
VMEM_LIMIT = 56 * 1024 * 1024


def _rms(x, g, eps):
    return x * lax.rsqrt(jnp.mean(x * x, axis=-1, keepdims=True) + eps) * g


def _nt_dot(a, b):
    return lax.dot_general(a, b, (((1,), (1,)), ((), ())), preferred_element_type=F32)


def _params(n_axes):
    return pltpu.CompilerParams(dimension_semantics=("arbitrary",) * n_axes, vmem_limit_bytes=VMEM_LIMIT)


def _da_qkv_kernel(x_ref, g_ref, wq_ref, wk_ref, wvt_ref, q_ref, k_ref, vt_ref, *, q_scale):
    xn = _rms(x_ref[...], g_ref[...], NORM_EPS).astype(MXU_DTYPE)
    q_ref[...] = (jnp.dot(xn, wq_ref[...], preferred_element_type=F32) * q_scale).astype(q_ref.dtype)
    k_ref[...] = jnp.dot(xn, wk_ref[...], preferred_element_type=F32).astype(k_ref.dtype)
    vt_ref[0] = _nt_dot(wvt_ref[...], xn).astype(vt_ref.dtype)


def _da_qkv(x, g, wq, wk, wvt, q_scale):
    n, d = x.shape
    tm = TOKEN_TILE
    full = lambda a: pl.BlockSpec(a.shape, lambda i: (0,) * a.ndim)
    return pl.pallas_call(
        functools.partial(_da_qkv_kernel, q_scale=q_scale),
        grid=(n // tm,),
        in_specs=[pl.BlockSpec((tm, d), lambda i: (i, 0)), full(g), full(wq), full(wk), full(wvt)],
        out_specs=[pl.BlockSpec((tm, d), lambda i: (i, 0)), pl.BlockSpec((tm, d), lambda i: (i, 0)),
                   pl.BlockSpec((1, d, tm), lambda i: (i, 0, 0))],
        out_shape=[jax.ShapeDtypeStruct((n, d), MXU_DTYPE), jax.ShapeDtypeStruct((n, d), MXU_DTYPE),
                   jax.ShapeDtypeStruct((n // tm, d, tm), MXU_DTYPE)],
        compiler_params=_params(1), name="da_qkv",
    )(x, g, wq, wk, wvt)


def _mla_proj_kernel(x_ref, g_ref, wd_ref, qg_ref, kvg_ref, wuq_ref, wkn_ref, wvt_ref, cos_ref, sin_ref,
                     q_ref, kn_ref, kr_ref, vt_ref, *, q_scale):
    xn = _rms(x_ref[...], g_ref[...], NORM_EPS).astype(MXU_DTYPE)
    down = jnp.dot(xn, wd_ref[...], preferred_element_type=F32)
    qr, kvr = MLA_Q_RANK, MLA_KV_RANK
    c_q = _rms(down[:, :qr], qg_ref[...], NORM_EPS).astype(MXU_DTYPE)
    c_kv = _rms(down[:, qr:qr + kvr], kvg_ref[...], NORM_EPS).astype(MXU_DTYPE)
    cos, sin = cos_ref[...], sin_ref[...]
    kr = down[:, qr + kvr:qr + kvr + LANES] * cos + down[:, qr + kvr + LANES:] * sin
    kr_ref[...] = kr.astype(kr_ref.dtype)
    hd = MLA_HEADS * LANES
    q = jnp.dot(c_q, wuq_ref[...], preferred_element_type=F32)
    q_ref[:, :hd] = (q[:, :hd] * q_scale).astype(q_ref.dtype)
    for h in range(MLA_HEADS):
        a = q[:, hd + h * LANES:hd + (h + 1) * LANES]
        b = q[:, 2 * hd + h * LANES:2 * hd + (h + 1) * LANES]
        q_ref[:, hd + h * LANES:hd + (h + 1) * LANES] = ((a * cos + b * sin) * q_scale).astype(q_ref.dtype)
    kn_ref[...] = jnp.dot(c_kv, wkn_ref[...], preferred_element_type=F32).astype(kn_ref.dtype)
    vt_ref[0] = _nt_dot(wvt_ref[...], c_kv).astype(vt_ref.dtype)


def _mla_proj(x, g, wd, qg, kvg, wuq, wkn, wvt, cos, sin, q_scale):
    n, d = x.shape
    tm = TOKEN_TILE
    hd = MLA_HEADS * LANES
    full = lambda a: pl.BlockSpec(a.shape, lambda i: (0,) * a.ndim)
    row = lambda w: pl.BlockSpec((tm, w), lambda i: (i, 0))
    return pl.pallas_call(
        functools.partial(_mla_proj_kernel, q_scale=q_scale),
        grid=(n // tm,),
        in_specs=[row(d), full(g), full(wd), full(qg), full(kvg), full(wuq), full(wkn), full(wvt),
                  row(LANES), row(LANES)],
        out_specs=[row(2 * hd), row(hd), row(LANES), pl.BlockSpec((1, hd, tm), lambda i: (i, 0, 0))],
        out_shape=[jax.ShapeDtypeStruct((n, 2 * hd), MXU_DTYPE), jax.ShapeDtypeStruct((n, hd), MXU_DTYPE),
                   jax.ShapeDtypeStruct((n, LANES), MXU_DTYPE),
                   jax.ShapeDtypeStruct((n // tm, hd, tm), MXU_DTYPE)],
        compiler_params=_params(1), name="mla_proj",
    )(x, g, wd, qg, kvg, wuq, wkn, wvt, cos, sin)


def _out_proj_kernel(x_ref, oa_ref, ob_ref, w_ref, y_ref, *, a_tiles):
    @pl.when(pl.program_id(0) < a_tiles)
    def _():
        y_ref[...] = x_ref[...] + jnp.dot(oa_ref[...], w_ref[...], preferred_element_type=F32)

    @pl.when(pl.program_id(0) >= a_tiles)
    def _():
        y_ref[...] = x_ref[...] + jnp.dot(ob_ref[...], w_ref[...], preferred_element_type=F32)


def _out_proj(x, oa, ob, w):
    n, d = x.shape
    tm = TOKEN_TILE
    a_tiles = oa.shape[0] // tm
    assert oa.shape[0] % tm == 0 and ob.shape[0] % tm == 0 and oa.shape[0] + ob.shape[0] == n
    return pl.pallas_call(
        functools.partial(_out_proj_kernel, a_tiles=a_tiles),
        grid=(n // tm,),
        in_specs=[pl.BlockSpec((tm, d), lambda i: (i, 0)),
                  pl.BlockSpec((tm, oa.shape[1]), lambda i: (jnp.minimum(i, a_tiles - 1), 0)),
                  pl.BlockSpec((tm, ob.shape[1]), lambda i: (jnp.maximum(i - a_tiles, 0), 0)),
                  pl.BlockSpec(w.shape, lambda i: (0, 0))],
        out_specs=pl.BlockSpec((tm, d), lambda i: (i, 0)),
        out_shape=jax.ShapeDtypeStruct((n, d), F32),
        compiler_params=_params(1), name="out_proj",
    )(x, oa, ob, w)


FLASH_STRIP = 32
FRAME_HEADROOM = 60.0
FLASH_ROUNDS = 2
FLASH_SLOTS = 3


def _flash_scratch(tk, w, dv):
    n = FLASH_SLOTS
    return ([pltpu.VMEM((tk, w), F32)] * n + [pltpu.VMEM((tk, w), MXU_DTYPE)] * n + [pltpu.VMEM((1, w), F32)] * (2 * n)
            + [pltpu.VMEM((1, w), F32), pltpu.VMEM((1, w), F32), pltpu.VMEM((dv, w), F32)])


def _flash_t(q, k_ref, vt_ref, n_chunks, tk, scratch, bias_fn, cols, fixed_frame):
    n = FLASH_SLOTS
    s_bufs, p_bufs, c_bufs, a_bufs = (scratch[i * n:(i + 1) * n] for i in range(4))
    m_ref, l_ref, acc_ref = scratch[4 * n:4 * n + 3]
    assert n_chunks >= 2
    w = q.shape[0]
    groups = [slice(g, g + cols) for g in range(0, w, cols)]

    def scores_g(c, slot, g):
        k = k_ref[pl.ds(pl.multiple_of(c * tk, tk), tk), :]
        st = _nt_dot(k, q[g, :])
        if bias_fn is not None:
            st = bias_fn(st, c)
        cmax = None
        for r in range(0, tk, FLASH_STRIP):
            strip = st[r:r + FLASH_STRIP, :]
            s_bufs[slot][r:r + FLASH_STRIP, g] = strip
            cmax = strip if cmax is None else jnp.maximum(cmax, strip)
        c_bufs[slot][:, g] = jnp.max(cmax, axis=0, keepdims=True)

    def softmax_g(slot, g):
        m_prev = m_ref[:, g]
        m_new = jnp.maximum(m_prev, c_bufs[slot][:, g])
        alpha = jnp.exp2(m_prev - m_new)
        m_ref[:, g] = m_new
        a_bufs[slot][:, g] = alpha
        psum = None
        for r in range(0, tk, FLASH_STRIP):
            p = jnp.exp2(s_bufs[slot][r:r + FLASH_STRIP, g] - m_new)
            psum = p if psum is None else psum + p
            p_bufs[slot][r:r + FLASH_STRIP, g] = p.astype(MXU_DTYPE)
        l_ref[:, g] = alpha * l_ref[:, g] + jnp.sum(psum, axis=0, keepdims=True)

    def values_g(c, slot, g):
        tv = vt_ref.shape[2]
        pv = None
        for j in range(tk // tv):
            part = jnp.dot(vt_ref[c * (tk // tv) + j], p_bufs[slot][j * tv:(j + 1) * tv, g],
                           preferred_element_type=F32)
            pv = part if pv is None else pv + part
        acc_ref[:, g] = acc_ref[:, g] * a_bufs[slot][:, g] + pv

    def scores(c, slot):
        for g in groups:
            scores_g(c, slot, g)

    def softmax(slot):
        for g in groups:
            softmax_g(slot, g)

    def values(c, slot):
        for g in groups:
            values_g(c, slot, g)

    def step(c, slot):
        for g in groups:
            values_g(c - 1, (slot - 1) % n, g)
            scores_g(c + 1, (slot + 1) % n, g)
            softmax_g(slot, g)

    per_trip = n * FLASH_ROUNDS

    def rescaling_pass():
        m_ref[...] = jnp.full(m_ref.shape, -jnp.inf, F32)
        l_ref[...] = jnp.zeros(l_ref.shape, F32)
        acc_ref[...] = jnp.zeros(acc_ref.shape, F32)
        scores(0, 0)
        scores(1, 1)
        softmax(0)
        trips = (n_chunks - 2) // per_trip

        def body(j, carry):
            for r in range(per_trip):
                step(1 + per_trip * j + r, (1 + r) % n)
            return carry

        lax.fori_loop(0, trips, body, 0)
        for c in range(1 + per_trip * trips, n_chunks - 1):
            step(c, c % n)
        values(n_chunks - 2, (n_chunks - 2) % n)
        softmax((n_chunks - 1) % n)
        values(n_chunks - 1, (n_chunks - 1) % n)

    if not fixed_frame:
        rescaling_pass()
        return

    excess_ref = c_bufs[0]

    def frame_values_g(c, slot, g):
        tv = vt_ref.shape[2]
        pv = None
        for j in range(tk // tv):
            part = jnp.dot(vt_ref[c * (tk // tv) + j], p_bufs[slot][j * tv:(j + 1) * tv, g],
                           preferred_element_type=F32)
            pv = part if pv is None else pv + part
        acc_ref[:, g] = acc_ref[:, g] + pv

    def frame_scores_g(c, slot, g):
        k = k_ref[pl.ds(pl.multiple_of(c * tk, tk), tk), :]
        st = _nt_dot(k, q[g, :])
        if bias_fn is not None:
            st = bias_fn(st, c)
        frame = m_ref[:, g]
        cmax = psum = None
        for r in range(0, tk, FLASH_STRIP):
            strip = st[r:r + FLASH_STRIP, :]
            cmax = strip if cmax is None else jnp.maximum(cmax, strip)
            p = jnp.exp2(strip - frame)
            psum = p if psum is None else psum + p
            p_bufs[slot][r:r + FLASH_STRIP, g] = p.astype(MXU_DTYPE)
        excess_ref[:, g] = jnp.maximum(excess_ref[:, g], jnp.max(cmax, axis=0, keepdims=True) - frame)
        l_ref[:, g] = l_ref[:, g] + jnp.sum(psum, axis=0, keepdims=True)

    def frame_step(c, slot):
        for g in groups:
            frame_values_g(c - 1, 1 - slot, g)
            frame_scores_g(c, slot, g)

    acc_ref[...] = jnp.zeros(acc_ref.shape, F32)
    l_ref[...] = jnp.zeros(l_ref.shape, F32)
    excess_ref[...] = jnp.zeros(excess_ref.shape, F32)
    for g in groups:
        k = k_ref[0:tk, :]
        st = _nt_dot(k, q[g, :])
        if bias_fn is not None:
            st = bias_fn(st, 0)
        frame = jnp.max(st, axis=0, keepdims=True)
        m_ref[:, g] = frame
        p = jnp.exp2(st - frame)
        l_ref[:, g] = jnp.sum(p, axis=0, keepdims=True)
        p_bufs[0][:, g] = p.astype(MXU_DTYPE)
    frame_trips = (n_chunks - 1) // per_trip

    def frame_body(j, carry):
        for r in range(per_trip):
            frame_step(1 + per_trip * j + r, (1 + r) % 2)
        return carry

    assert per_trip % 2 == 0
    lax.fori_loop(0, frame_trips, frame_body, 0)
    for c in range(1 + per_trip * frame_trips, n_chunks):
        frame_step(c, c % 2)
    for g in groups:
        frame_values_g(n_chunks - 1, (n_chunks - 1) % 2, g)

    @pl.when(jnp.max(excess_ref[...]) > FRAME_HEADROOM)
    def _():
        rescaling_pass()


def _kv_chunk(seq):
    return min(KV_CHUNK, max(TOKEN_TILE, seq // 2))


def _da_attn_kernel(sc_ref, q_ref, k_ref, vt_ref, g_ref, o_ref, dist_ref, *scratch,
                    tq, tk, n_chunks, n_heads, out_scale):
    l_ref, acc_ref = scratch[4 * FLASH_SLOTS + 1:]
    h, qi = pl.program_id(1), pl.program_id(2)
    slope, lam = sc_ref[h], sc_ref[n_heads]
    q = q_ref[...]
    lane = lax.broadcasted_iota(jnp.int32, q.shape, 1)
    zero = jnp.zeros_like(q)
    d = q.shape[1] // 2
    qs = jnp.concatenate([jnp.where(lane < d, q, zero), jnp.where(lane >= d, q, zero)], axis=0)
    row = lax.broadcasted_iota(jnp.int32, (tk, tq), 0)
    col = lax.broadcasted_iota(jnp.int32, (tk, tq), 1)
    dist_ref[...] = (row - col - qi * tq).astype(F32)

    def bias_fn(st, c):
        b = slope * jnp.abs(dist_ref[...] + lax.convert_element_type(c * tk, F32))
        return jnp.concatenate([st[:, :tq] - b, st[:, tq:] - b], axis=1)

    _flash_t(qs, k_ref, vt_ref, n_chunks, tk, scratch, bias_fn, 2 * tq, fixed_frame=False)
    o = acc_ref[...] * (1.0 / l_ref[...])
    o = o[:, :tq] - lam * o[:, tq:]
    o = o * lax.rsqrt(jnp.mean(o * o, axis=0, keepdims=True) + DA_SUBLN_EPS) * g_ref[...] * out_scale
    o_ref[...] = o.T.astype(o_ref.dtype)


def _da_attn(scalars, q, k, vt, subln_col, batch, seq, row0, out_scale):
    dm = q.shape[1]
    tq, tk, tv = DA_Q_TILE, _kv_chunk(seq), vt.shape[2]
    hw = 2 * DA_HEAD_DIM
    n_heads = dm // hw
    assert row0 % seq == 0 and seq % tq == 0 and seq % tk == 0 and tk % tv == 0
    s0, q0, nq, n_chunks = row0 // seq, row0 // tq, seq // tq, seq // tk
    return pl.pallas_call(
        functools.partial(_da_attn_kernel, tq=tq, tk=tk, n_chunks=n_chunks, n_heads=n_heads, out_scale=out_scale),
        grid_spec=pltpu.PrefetchScalarGridSpec(
            num_scalar_prefetch=1, grid=(batch, n_heads, nq),
            in_specs=[pl.BlockSpec((tq, hw), lambda b, h, i, sc: (q0 + b * nq + i, h)),
                      pl.BlockSpec((seq, hw), lambda b, h, i, sc: (s0 + b, h)),
                      pl.BlockSpec((seq // tv, hw, tv), lambda b, h, i, sc: (s0 + b, h, 0)),
                      pl.BlockSpec(subln_col.shape, lambda b, h, i, sc: (0, 0))],
            out_specs=pl.BlockSpec((tq, hw), lambda b, h, i, sc: (b * nq + i, h)),
            scratch_shapes=[pltpu.VMEM((tk, tq), F32)] + _flash_scratch(tk, 2 * tq, hw)),
        out_shape=jax.ShapeDtypeStruct((batch * seq, dm), MXU_DTYPE),
        compiler_params=_params(3), name="da_attn",
    )(scalars, q, k, vt, subln_col)


def _mla_attn_kernel(qn_ref, qr_ref, kn_ref, kr_ref, vt_ref, o_ref, kfull_ref, *scratch, tk, n_chunks):
    l_ref, acc_ref = scratch[4 * FLASH_SLOTS + 1:]

    @pl.when(pl.program_id(2) == 0)
    def _():
        def copy(c, carry):
            rows = pl.ds(pl.multiple_of(c * tk, tk), tk)
            kfull_ref[rows, :LANES] = kn_ref[rows, :]
            kfull_ref[rows, LANES:] = kr_ref[rows, :]
            return carry
        lax.fori_loop(0, n_chunks, copy, 0)

    q = jnp.concatenate([qn_ref[...], qr_ref[...]], axis=1)
    _flash_t(q, kfull_ref, vt_ref, n_chunks, tk, scratch, None, MLA_COLS, fixed_frame=True)
    o = acc_ref[...] * (1.0 / l_ref[...])
    o_ref[...] = o.T.astype(o_ref.dtype)


def _mla_attn(q, kn, kr, vt, batch, seq, row0):
    n_heads = MLA_HEADS
    tq, tk, tv = MLA_Q_TILE, _kv_chunk(seq), vt.shape[2]
    assert row0 % seq == 0 and seq % tq == 0 and seq % tk == 0 and tk % tv == 0
    s0, q0, nq, n_chunks = row0 // seq, row0 // tq, seq // tq, seq // tk
    return pl.pallas_call(
        functools.partial(_mla_attn_kernel, tk=tk, n_chunks=n_chunks),
        grid=(batch, n_heads, nq),
        in_specs=[pl.BlockSpec((tq, LANES), lambda b, h, i: (q0 + b * nq + i, h)),
                  pl.BlockSpec((tq, LANES), lambda b, h, i: (q0 + b * nq + i, n_heads + h)),
                  pl.BlockSpec((seq, LANES), lambda b, h, i: (s0 + b, h)),
                  pl.BlockSpec((seq, LANES), lambda b, h, i: (s0 + b, 0)),
                  pl.BlockSpec((seq // tv, MLA_V, tv), lambda b, h, i: (s0 + b, h, 0))],
        out_specs=pl.BlockSpec((tq, MLA_V), lambda b, h, i: (b * nq + i, h)),
        out_shape=jax.ShapeDtypeStruct((batch * seq, n_heads * MLA_V), MXU_DTYPE),
        scratch_shapes=[pltpu.VMEM((seq, 2 * LANES), MXU_DTYPE)] + _flash_scratch(tk, tq, MLA_V),
        compiler_params=_params(3), name="mla_attn",
    )(q, q, kn, kr, vt)


R_E0, R_E1, R_RANK0, R_RANK1, R_GATE0, R_GATE1 = range(6)


def _router_kernel(x_ref, g_ref, w_ref, tri_ref, rec_ref, cnt_ref, base_ref):
    i = pl.program_id(0)

    @pl.when(i == 0)
    def _():
        base_ref[...] = jnp.zeros(base_ref.shape, F32)

    xn = _rms(x_ref[...], g_ref[...], NORM_EPS)
    logits = jnp.dot(xn, w_ref[...], preferred_element_type=F32, precision=lax.Precision.HIGHEST)
    tm = logits.shape[0]
    lane = lax.broadcasted_iota(jnp.int32, logits.shape, 1)
    neg = jnp.float32(-jnp.inf)

    def first_lane(mask):
        return jnp.min(jnp.where(mask, lane, LANES), axis=1, keepdims=True)

    is_g = lane < N_GROUPS
    lg = jnp.where(is_g, logits, neg)
    mg = jnp.max(lg, axis=1, keepdims=True)
    pg_sel = 1.0 / jnp.sum(jnp.exp(lg - mg), axis=1, keepdims=True)
    g_sel = first_lane(lg == mg)
    e_lo = N_GROUPS + EXPERTS_PER_GROUP * g_sel
    in_e = (lane >= e_lo) & (lane < e_lo + EXPERTS_PER_GROUP)
    le = jnp.where(in_e, logits, neg)
    me = jnp.max(le, axis=1, keepdims=True)
    ee = jnp.exp(le - me)
    pe = ee / jnp.sum(ee, axis=1, keepdims=True)
    pe = jnp.where(in_e, pe, -1.0)
    p0 = jnp.max(pe, axis=1, keepdims=True)
    i0 = first_lane(pe == p0)
    pe1 = jnp.where(lane == i0, -1.0, pe)
    p1 = jnp.max(pe1, axis=1, keepdims=True)
    i1 = first_lane(pe1 == p1)
    gate0 = pg_sel * p0 / (p0 + p1)
    gate1 = pg_sel * p1 / (p0 + p1)
    hit0, hit1 = lane == i0, lane == i1
    oh = jnp.concatenate([jnp.where(hit0, 1.0, 0.0), jnp.where(hit1, 1.0, 0.0)], axis=1)
    before = jnp.dot(tri_ref[...], oh.astype(MXU_DTYPE), preferred_element_type=F32)
    base = base_ref[...]
    tot0 = jnp.sum(oh[:, :LANES], axis=0, keepdims=True)
    tot1 = jnp.sum(oh[:, LANES:], axis=0, keepdims=True)
    rank0 = jnp.sum(jnp.where(hit0, before[:, :LANES] + base, 0.0), axis=1, keepdims=True)
    rank1 = jnp.sum(jnp.where(hit1, before[:, LANES:] + (base + tot0), 0.0), axis=1, keepdims=True)
    base = base + tot0 + tot1
    base_ref[...] = base
    cnt_ref[...] = base
    rec = jnp.zeros(logits.shape, F32)
    for pos, val in ((R_E0, (i0 - N_GROUPS).astype(F32)), (R_E1, (i1 - N_GROUPS).astype(F32)),
                     (R_RANK0, rank0), (R_RANK1, rank1), (R_GATE0, gate0), (R_GATE1, gate1)):
        rec = jnp.where(lane == pos, val, rec)
    rec_ref[...] = rec


def _router(x, g, w_router, tri):
    n, d = x.shape
    tm = TOKEN_TILE
    full = lambda a: pl.BlockSpec(a.shape, lambda i: (0,) * a.ndim)
    return pl.pallas_call(
        _router_kernel,
        grid=(n // tm,),
        in_specs=[pl.BlockSpec((tm, d), lambda i: (i, 0)), full(g), full(w_router), full(tri)],
        out_specs=[pl.BlockSpec((tm, LANES), lambda i: (i, 0)), pl.BlockSpec((1, LANES), lambda i: (0, 0))],
        out_shape=[jax.ShapeDtypeStruct((n, LANES), F32), jax.ShapeDtypeStruct((1, LANES), F32)],
        scratch_shapes=[pltpu.VMEM((1, LANES), F32)],
        compiler_params=_params(1), name="moe_router",
    )(x, g, w_router, tri)


ROW_TILE = 8


def _row_copy(src_ref, src_row, dst_ref, dst_row, sem):
    src = src_ref.at[pl.ds(pl.multiple_of(src_row * ROW_TILE, ROW_TILE), ROW_TILE), :]
    dst = dst_ref.at[pl.ds(pl.multiple_of(dst_row * ROW_TILE, ROW_TILE), ROW_TILE), :]
    return pltpu.make_async_copy(src, dst, sem)


def _to_row_tiles(ref, x):
    rows = x.shape[0]
    for c in range(ROW_TILE):
        ref[pl.ds(c, rows, stride=ROW_TILE), :] = x[:, c * LANES:(c + 1) * LANES]


def _from_row_tiles(ref):
    rows = ref.shape[0] // ROW_TILE
    return jnp.concatenate([ref[pl.ds(c, rows, stride=ROW_TILE), :] for c in range(ROW_TILE)], axis=1)


def _dispatch_kernel(dest_ref, x_ref, g_ref, xs_in_ref, xs_ref, xn_ref, sem, *, tm):
    del xs_in_ref
    _to_row_tiles(xn_ref, _rms(x_ref[...], g_ref[...], NORM_EPS))

    def issue(r, carry):
        for c in range(2):
            _row_copy(xn_ref, r, xs_ref, dest_ref[2 * r + c], sem).start()
        return carry
    lax.fori_loop(0, tm, issue, 0, unroll=8)

    def drain(r, carry):
        for c in range(2):
            _row_copy(xn_ref, 0, xs_ref, 0, sem).wait()
        return carry
    lax.fori_loop(0, tm, drain, 0, unroll=8)


def _dispatch(dest_flat, x, g, xs_zero):
    n, d = x.shape
    tm = TOKEN_TILE
    return pl.pallas_call(
        functools.partial(_dispatch_kernel, tm=tm),
        grid=(n // tm,),
        in_specs=[pl.BlockSpec((2 * tm,), lambda i: (i,), memory_space=pltpu.SMEM),
                  pl.BlockSpec((tm, d), lambda i: (i, 0)), pl.BlockSpec(g.shape, lambda i: (0, 0)),
                  pl.BlockSpec(memory_space=pl.ANY)],
        out_specs=pl.BlockSpec(memory_space=pl.ANY),
        out_shape=jax.ShapeDtypeStruct(xs_zero.shape, F32),
        scratch_shapes=[pltpu.VMEM((tm * ROW_TILE, LANES), F32), pltpu.SemaphoreType.DMA(())],
        input_output_aliases={3: 0},
        compiler_params=_params(1), name="moe_dispatch",
    )(dest_flat, x, g, xs_zero)


def _expert_kernel(be_ref, nv_ref, xs_ref, w1_ref, w3_ref, w2_ref, y_ref, w1c_ref, w3c_ref, w2c_ref):
    b = pl.program_id(0)

    @pl.when((b == 0) | (be_ref[b] != be_ref[jnp.maximum(b - 1, 0)]))
    def _():
        w1c_ref[...] = w1_ref[0, 0].astype(MXU_DTYPE)
        w3c_ref[...] = w3_ref[0, 0].astype(MXU_DTYPE)
        w2c_ref[...] = w2_ref[0, 0].astype(MXU_DTYPE)

    @pl.when(b < nv_ref[0])
    def _():
        xb = _from_row_tiles(xs_ref).astype(MXU_DTYPE)
        h1 = jnp.dot(xb, w1c_ref[...], preferred_element_type=F32)
        h3 = jnp.dot(xb, w3c_ref[...], preferred_element_type=F32)
        h = (h1 * jax.nn.sigmoid(h1) * h3).astype(MXU_DTYPE)
        _to_row_tiles(y_ref, jnp.dot(h, w2c_ref[...], preferred_element_type=F32))

    @pl.when(b >= nv_ref[0])
    def _():
        y_ref[...] = jnp.zeros(y_ref.shape, F32)


def _experts(block_expert, n_valid, xs, w1, w3, w2, layer):
    rb = EXPERT_ROWS
    d, de = w1.shape[2:]
    assert d == ROW_TILE * LANES
    return pl.pallas_call(
        _expert_kernel,
        grid_spec=pltpu.PrefetchScalarGridSpec(
            num_scalar_prefetch=2, grid=(xs.shape[0] // (rb * ROW_TILE),),
            in_specs=[pl.BlockSpec((rb * ROW_TILE, LANES), lambda b, be, nv: (b, 0)),
                      pl.BlockSpec((1, 1, d, de), lambda b, be, nv: (layer, be[b], 0, 0)),
                      pl.BlockSpec((1, 1, d, de), lambda b, be, nv: (layer, be[b], 0, 0)),
                      pl.BlockSpec((1, 1, de, d), lambda b, be, nv: (layer, be[b], 0, 0))],
            out_specs=pl.BlockSpec((rb * ROW_TILE, LANES), lambda b, be, nv: (b, 0)),
            scratch_shapes=[pltpu.VMEM((d, de), MXU_DTYPE), pltpu.VMEM((d, de), MXU_DTYPE),
                            pltpu.VMEM((de, d), MXU_DTYPE)]),
        out_shape=jax.ShapeDtypeStruct(xs.shape, F32),
        compiler_params=_params(1), name="moe_experts",
    )(block_expert, n_valid, xs, w1, w3, w2)


def _combine_kernel(dest_ref, x_ref, rec_ref, gfin_ref, yb_ref, o_ref, ybuf_ref, sem, *, tm, final):
    def issue(r, carry):
        for c in range(2):
            _row_copy(yb_ref, dest_ref[2 * r + c], ybuf_ref.at[c], r, sem).start()
        return carry
    lax.fori_loop(0, tm, issue, 0, unroll=8)

    def drain(r, carry):
        for c in range(2):
            _row_copy(yb_ref, 0, ybuf_ref.at[c], 0, sem).wait()
        return carry
    lax.fori_loop(0, tm, drain, 0, unroll=8)

    rec = rec_ref[...]
    out = (x_ref[...] + rec[:, R_GATE0:R_GATE0 + 1] * _from_row_tiles(ybuf_ref.at[0])
           + rec[:, R_GATE1:R_GATE1 + 1] * _from_row_tiles(ybuf_ref.at[1]))
    if final:
        out = _rms(out, gfin_ref[...], NORM_EPS)
    o_ref[...] = out


def _combine(dest_flat, x, rec, g_final, yb, final):
    n, d = x.shape
    tm = TOKEN_TILE
    return pl.pallas_call(
        functools.partial(_combine_kernel, tm=tm, final=final),
        grid=(n // tm,),
        in_specs=[pl.BlockSpec((2 * tm,), lambda i: (i,), memory_space=pltpu.SMEM),
                  pl.BlockSpec((tm, d), lambda i: (i, 0)), pl.BlockSpec((tm, LANES), lambda i: (i, 0)),
                  pl.BlockSpec(g_final.shape, lambda i: (0, 0)), pl.BlockSpec(memory_space=pl.ANY)],
        out_specs=pl.BlockSpec((tm, d), lambda i: (i, 0)),
        out_shape=jax.ShapeDtypeStruct((n, d), F32),
        scratch_shapes=[pltpu.VMEM((2, tm * ROW_TILE, LANES), F32), pltpu.SemaphoreType.DMA(())],
        compiler_params=_params(1), name="moe_combine",
    )(dest_flat, x, rec, g_final, yb)


def _moe(x, g, w_router, tri, w1, w3, w2, layer, g_final, final):
    n, d = x.shape
    rb = EXPERT_ROWS
    rec, counts = _router(x, g, w_router, tri)
    counts = counts[0, N_GROUPS:N_GROUPS + N_EXPERTS].astype(jnp.int32)
    padded = (counts + rb - 1) // rb * rb
    pad_end = jnp.cumsum(padded)
    pad_start = pad_end - padded
    n_blocks = (2 * n + N_EXPERTS * (rb - 1)) // rb
    expert = rec[:, R_E0:R_E1 + 1].astype(jnp.int32)
    rank = rec[:, R_RANK0:R_RANK1 + 1].astype(jnp.int32)
    before = expert[..., None] > jnp.arange(N_EXPERTS, dtype=jnp.int32)
    dest = (jnp.sum(jnp.where(before, padded, 0), axis=-1) + rank).reshape(-1)
    block_start = jnp.arange(n_blocks, dtype=jnp.int32) * rb
    block_expert = jnp.minimum(jnp.searchsorted(pad_end, block_start, side="right"), N_EXPERTS - 1).astype(jnp.int32)
    n_valid = (pad_end[-1:] // rb).astype(jnp.int32)
    xs = _dispatch(dest, x, g, jnp.zeros((n_blocks * rb * ROW_TILE, LANES), F32))
    yb = _experts(block_expert, n_valid, xs, w1, w3, w2, layer)
    return _combine(dest, x, rec, g_final, yb, final)


def _rope_tables(positions):
    inv = 1.0 / (ROPE_THETA ** (jnp.arange(0, MLA_ROPE, 2, dtype=F32) / MLA_ROPE))
    ang = positions.astype(F32)[:, None] * inv[None, :]
    cos = jnp.repeat(jnp.cos(ang), 2, axis=1)
    sin = jnp.stack([-jnp.sin(ang), jnp.sin(ang)], axis=-1).reshape(ang.shape[0], MLA_ROPE)
    pad = ((0, 0), (0, LANES - MLA_ROPE))
    return jnp.pad(cos, pad), jnp.pad(sin, pad)


def _pair_swap(w):
    return w.reshape(w.shape[0], -1, 2)[:, :, ::-1].reshape(w.shape)


def kernel(x_prompt, x_sample, ln_mix, ln_ffn, ln_final, da_wqkv, da_wo, da_lambda_q1, da_lambda_k1, da_lambda_q2, da_lambda_k2, da_subln, mla_w_down, mla_q_norm, mla_w_uq, mla_kv_norm, mla_w_ukv, mla_wo, moe_w_group, moe_w_expert, moe_w1, moe_w3, moe_w2):
    bp, sp, d = x_prompt.shape
    bs, ss, _ = x_sample.shape
    n_p, n_s = bp * sp, bs * ss
    groups = ((bp, sp, 0), (bs, ss, n_p))
    x = jnp.concatenate([x_prompt.reshape(n_p, d), x_sample.reshape(n_s, d)], axis=0)
    depth = ln_mix.shape[0]
    cdt = MXU_DTYPE

    positions = jnp.concatenate([jnp.tile(jnp.arange(sp), bp), jnp.tile(jnp.arange(ss), bs)])
    cos, sin = _rope_tables(positions)
    tri = jnp.tril(jnp.ones((TOKEN_TILE, TOKEN_TILE), F32), -1).astype(cdt)
    n_da_heads = d // (2 * DA_HEAD_DIM)
    slopes = 2.0 ** (-8.0 * np.arange(1, n_da_heads + 1) / n_da_heads) * LOG2E
    hd = MLA_HEADS * LANES
    zpad = LANES - MLA_ROPE

    for i in range(depth):
        j = i // 2
        g_mix = ln_mix[i][None, :]
        if i % 2 == 0:
            lambda_init = 0.8 - 0.6 * math.exp(-0.3 * i)
            lam = (jnp.exp(jnp.sum(da_lambda_q1[j] * da_lambda_k1[j])) - jnp.exp(jnp.sum(da_lambda_q2[j] * da_lambda_k2[j]))
                   + lambda_init)
            scalars = jnp.concatenate([jnp.asarray(slopes, F32), lam[None].astype(F32)])
            w = da_wqkv[j]
            q, k, vt = _da_qkv(x, g_mix, w[:, :d].astype(cdt), w[:, d:2 * d].astype(cdt), w[:, 2 * d:].T.astype(cdt),
                               DA_HEAD_DIM ** -0.5 * LOG2E)
            oa, ob = [_da_attn(scalars, q, k, vt, da_subln[j][:, None], batch, seq, row0, 1.0 - lambda_init)
                      for batch, seq, row0 in groups]
            x = _out_proj(x, oa, ob, da_wo[j].astype(cdt))
        else:
            qr, kvr = MLA_Q_RANK, MLA_KV_RANK
            wd = mla_w_down[j]
            w_rope = wd[:, qr + kvr:]
            wd = jnp.concatenate([wd[:, :qr + kvr], jnp.pad(w_rope, ((0, 0), (0, zpad))),
                                  jnp.pad(_pair_swap(w_rope), ((0, 0), (0, zpad)))], axis=1)
            wuq = mla_w_uq[j].reshape(qr, MLA_HEADS, MLA_NOPE + MLA_ROPE)
            wq_rope = wuq[:, :, MLA_NOPE:]
            pad3 = ((0, 0), (0, 0), (0, zpad))
            wuq = jnp.concatenate([wuq[:, :, :MLA_NOPE].reshape(qr, hd), jnp.pad(wq_rope, pad3).reshape(qr, hd),
                                   jnp.pad(_pair_swap(wq_rope.reshape(qr, -1)).reshape(wq_rope.shape), pad3).reshape(qr, hd)],
                                  axis=1)
            wukv = mla_w_ukv[j].reshape(kvr, MLA_HEADS, MLA_NOPE + MLA_V)
            wkn = wukv[:, :, :MLA_NOPE].reshape(kvr, hd)
            wvt = wukv[:, :, MLA_NOPE:].reshape(kvr, MLA_HEADS * MLA_V).T
            q, kn, kr, vt = _mla_proj(x, g_mix, wd.astype(cdt), mla_q_norm[j][None, :], mla_kv_norm[j][None, :],
                                      wuq.astype(cdt), wkn.astype(cdt), wvt.astype(cdt), cos, sin,
                                      (MLA_NOPE + MLA_ROPE) ** -0.5 * LOG2E)
            oa, ob = [_mla_attn(q, kn, kr, vt, batch, seq, row0) for batch, seq, row0 in groups]
            x = _out_proj(x, oa, ob, mla_wo[j].astype(cdt))
        w_router = jnp.pad(jnp.concatenate([moe_w_group[i], moe_w_expert[i]], axis=1),
                           ((0, 0), (0, LANES - N_GROUPS - N_EXPERTS)))
        x = _moe(x, ln_ffn[i][None, :], w_router, tri, moe_w1, moe_w3, moe_w2, i, ln_final[None, :],
                 final=(i == depth - 1))
    return x[:n_p].reshape(bp, sp, d), x[n_p:].reshape(bs, ss, d)
```

```python
import functools
import math

import numpy as np
import jax
import jax.numpy as jnp
from jax import lax
from jax.experimental import pallas as pl
from jax.experimental.pallas import tpu as pltpu

F32 = jnp.float32
MXU_DTYPE = jnp.bfloat16
LOG2E = 1.4426950408889634

NORM_EPS = 1e-6
DA_SUBLN_EPS = 1e-5
ROPE_THETA = 10000.0
DA_HEAD_DIM = 64
MLA_HEADS = 8
MLA_NOPE = 128
MLA_ROPE = 64
MLA_V = 128
MLA_Q_RANK = 384
MLA_KV_RANK = 256
N_GROUPS = 8
EXPERTS_PER_GROUP = 8
N_EXPERTS = N_GROUPS * EXPERTS_PER_GROUP

LANES = 128
TOKEN_TILE = 512
KV_CHUNK = 512
DA_Q_TILE = 256
MLA_Q_TILE = 512
MLA_COLS = 256
EXPERT_ROWS = 256
VMEM_LIMIT = 56 * 1024 * 1024


def _rms(x, g, eps):
    return x * lax.rsqrt(jnp.mean(x * x, axis=-1, keepdims=True) + eps) * g


def _nt_dot(a, b):
    return lax.dot_general(a, b, (((1,), (1,)), ((), ())), preferred_element_type=F32)


def _params(n_axes):
    return pltpu.CompilerParams(dimension_semantics=("arbitrary",) * n_axes, vmem_limit_bytes=VMEM_LIMIT)


def _da_qkv_kernel(x_ref, g_ref, wq_ref, wk_ref, wvt_ref, q_ref, k_ref, vt_ref, *, q_scale):
    xn = _rms(x_ref[...], g_ref[...], NORM_EPS).astype(MXU_DTYPE)
    q_ref[...] = (jnp.dot(xn, wq_ref[...], preferred_element_type=F32) * q_scale).astype(q_ref.dtype)
    k_ref[...] = jnp.dot(xn, wk_ref[...], preferred_element_type=F32).astype(k_ref.dtype)
    vt_ref[0] = _nt_dot(wvt_ref[...], xn).astype(vt_ref.dtype)


def _da_qkv(x, g, wq, wk, wvt, q_scale):
    n, d = x.shape
    tm = TOKEN_TILE
    full = lambda a: pl.BlockSpec(a.shape, lambda i: (0,) * a.ndim)
    return pl.pallas_call(
        functools.partial(_da_qkv_kernel, q_scale=q_scale),
        grid=(n // tm,),
        in_specs=[pl.BlockSpec((tm, d), lambda i: (i, 0)), full(g), full(wq), full(wk), full(wvt)],
        out_specs=[pl.BlockSpec((tm, d), lambda i: (i, 0)), pl.BlockSpec((tm, d), lambda i: (i, 0)),
                   pl.BlockSpec((1, d, tm), lambda i: (i, 0, 0))],
        out_shape=[jax.ShapeDtypeStruct((n, d), MXU_DTYPE), jax.ShapeDtypeStruct((n, d), MXU_DTYPE),
                   jax.ShapeDtypeStruct((n // tm, d, tm), MXU_DTYPE)],
        compiler_params=_params(1), name="da_qkv",
    )(x, g, wq, wk, wvt)


def _mla_proj_kernel(x_ref, g_ref, wd_ref, qg_ref, kvg_ref, wuq_ref, wkn_ref, wvt_ref, cos_ref, sin_ref,
                     q_ref, kn_ref, kr_ref, vt_ref, *, q_scale):
    xn = _rms(x_ref[...], g_ref[...], NORM_EPS).astype(MXU_DTYPE)
    down = jnp.dot(xn, wd_ref[...], preferred_element_type=F32)
    qr, kvr = MLA_Q_RANK, MLA_KV_RANK
    c_q = _rms(down[:, :qr], qg_ref[...], NORM_EPS).astype(MXU_DTYPE)
    c_kv = _rms(down[:, qr:qr + kvr], kvg_ref[...], NORM_EPS).astype(MXU_DTYPE)
    cos, sin = cos_ref[...], sin_ref[...]
    kr = down[:, qr + kvr:qr + kvr + LANES] * cos + down[:, qr + kvr + LANES:] * sin
    kr_ref[...] = kr.astype(kr_ref.dtype)
    hd = MLA_HEADS * LANES
    q = jnp.dot(c_q, wuq_ref[...], preferred_element_type=F32)
    q_ref[:, :hd] = (q[:, :hd] * q_scale).astype(q_ref.dtype)
    for h in range(MLA_HEADS):
        a = q[:, hd + h * LANES:hd + (h + 1) * LANES]
        b = q[:, 2 * hd + h * LANES:2 * hd + (h + 1) * LANES]
        q_ref[:, hd + h * LANES:hd + (h + 1) * LANES] = ((a * cos + b * sin) * q_scale).astype(q_ref.dtype)
    kn_ref[...] = jnp.dot(c_kv, wkn_ref[...], preferred_element_type=F32).astype(kn_ref.dtype)
    vt_ref[0] = _nt_dot(wvt_ref[...], c_kv).astype(vt_ref.dtype)


def _mla_proj(x, g, wd, qg, kvg, wuq, wkn, wvt, cos, sin, q_scale):
    n, d = x.shape
    tm = TOKEN_TILE
    hd = MLA_HEADS * LANES
    full = lambda a: pl.BlockSpec(a.shape, lambda i: (0,) * a.ndim)
    row = lambda w: pl.BlockSpec((tm, w), lambda i: (i, 0))
    return pl.pallas_call(
        functools.partial(_mla_proj_kernel, q_scale=q_scale),
        grid=(n // tm,),
        in_specs=[row(d), full(g), full(wd), full(qg), full(kvg), full(wuq), full(wkn), full(wvt),
                  row(LANES), row(LANES)],
        out_specs=[row(2 * hd), row(hd), row(LANES), pl.BlockSpec((1, hd, tm), lambda i: (i, 0, 0))],
        out_shape=[jax.ShapeDtypeStruct((n, 2 * hd), MXU_DTYPE), jax.ShapeDtypeStruct((n, hd), MXU_DTYPE),
                   jax.ShapeDtypeStruct((n, LANES), MXU_DTYPE),
                   jax.ShapeDtypeStruct((n // tm, hd, tm), MXU_DTYPE)],
        compiler_params=_params(1), name="mla_proj",
    )(x, g, wd, qg, kvg, wuq, wkn, wvt, cos, sin)


def _out_proj_kernel(x_ref, oa_ref, ob_ref, w_ref, y_ref, *, a_tiles):
    @pl.when(pl.program_id(0) < a_tiles)
    def _():
        y_ref[...] = x_ref[...] + jnp.dot(oa_ref[...], w_ref[...], preferred_element_type=F32)

    @pl.when(pl.program_id(0) >= a_tiles)
    def _():
        y_ref[...] = x_ref[...] + jnp.dot(ob_ref[...], w_ref[...], preferred_element_type=F32)


def _out_proj(x, oa, ob, w):
    n, d = x.shape
    tm = TOKEN_TILE
    a_tiles = oa.shape[0] // tm
    assert oa.shape[0] % tm == 0 and ob.shape[0] % tm == 0 and oa.shape[0] + ob.shape[0] == n
    return pl.pallas_call(
        functools.partial(_out_proj_kernel, a_tiles=a_tiles),
        grid=(n // tm,),
        in_specs=[pl.BlockSpec((tm, d), lambda i: (i, 0)),
                  pl.BlockSpec((tm, oa.shape[1]), lambda i: (jnp.minimum(i, a_tiles - 1), 0)),
                  pl.BlockSpec((tm, ob.shape[1]), lambda i: (jnp.maximum(i - a_tiles, 0), 0)),
                  pl.BlockSpec(w.shape, lambda i: (0, 0))],
        out_specs=pl.BlockSpec((tm, d), lambda i: (i, 0)),
        out_shape=jax.ShapeDtypeStruct((n, d), F32),
        compiler_params=_params(1), name="out_proj",
    )(x, oa, ob, w)


FLASH_STRIP = 32
FRAME_HEADROOM = 60.0
FLASH_ROUNDS = 2
FLASH_SLOTS = 3


def _flash_scratch(tk, w, dv):
    n = FLASH_SLOTS
    return ([pltpu.VMEM((tk, w), F32)] * n + [pltpu.VMEM((tk, w), MXU_DTYPE)] * n + [pltpu.VMEM((1, w), F32)] * (2 * n)
            + [pltpu.VMEM((1, w), F32), pltpu.VMEM((1, w), F32), pltpu.VMEM((dv, w), F32)])


class _Frame:
    def __init__(self, order, operands, weighted):
        self.order, self.operands, self.weighted = order, operands, weighted


def _flash_t(q, k_ref, vt_ref, n_chunks, tk, scratch, bias_fn, cols, frame):
    n = FLASH_SLOTS
    s_bufs, p_bufs, c_bufs, a_bufs = (scratch[i * n:(i + 1) * n] for i in range(4))
    m_ref, l_ref, acc_ref = scratch[4 * n:4 * n + 3]
    assert n_chunks >= 2
    w = q.shape[0]
    groups = [slice(g, g + cols) for g in range(0, w, cols)]

    def scores_g(c, slot, g):
        k = k_ref[pl.ds(pl.multiple_of(c * tk, tk), tk), :]
        st = _nt_dot(k, q[g, :])
        if bias_fn is not None:
            st = bias_fn(st, c)
        cmax = None
        for r in range(0, tk, FLASH_STRIP):
            strip = st[r:r + FLASH_STRIP, :]
            s_bufs[slot][r:r + FLASH_STRIP, g] = strip
            cmax = strip if cmax is None else jnp.maximum(cmax, strip)
        c_bufs[slot][:, g] = jnp.max(cmax, axis=0, keepdims=True)

    def softmax_g(slot, g):
        m_prev = m_ref[:, g]
        m_new = jnp.maximum(m_prev, c_bufs[slot][:, g])
        alpha = jnp.exp2(m_prev - m_new)
        m_ref[:, g] = m_new
        a_bufs[slot][:, g] = alpha
        psum = None
        for r in range(0, tk, FLASH_STRIP):
            p = jnp.exp2(s_bufs[slot][r:r + FLASH_STRIP, g] - m_new)
            psum = p if psum is None else psum + p
            p_bufs[slot][r:r + FLASH_STRIP, g] = p.astype(MXU_DTYPE)
        l_ref[:, g] = alpha * l_ref[:, g] + jnp.sum(psum, axis=0, keepdims=True)

    def values_g(c, slot, g):
        tv = vt_ref.shape[2]
        pv = None
        for j in range(tk // tv):
            part = jnp.dot(vt_ref[c * (tk // tv) + j], p_bufs[slot][j * tv:(j + 1) * tv, g],
                           preferred_element_type=F32)
            pv = part if pv is None else pv + part
        acc_ref[:, g] = acc_ref[:, g] * a_bufs[slot][:, g] + pv

    def scores(c, slot):
        for g in groups:
            scores_g(c, slot, g)

    def softmax(slot):
        for g in groups:
            softmax_g(slot, g)

    def values(c, slot):
        for g in groups:
            values_g(c, slot, g)

    def step(c, slot):
        for g in groups:
            values_g(c - 1, (slot - 1) % n, g)
            scores_g(c + 1, (slot + 1) % n, g)
            softmax_g(slot, g)

    per_trip = n * FLASH_ROUNDS

    def rescaling_pass():
        m_ref[...] = jnp.full(m_ref.shape, -jnp.inf, F32)
        l_ref[...] = jnp.zeros(l_ref.shape, F32)
        acc_ref[...] = jnp.zeros(acc_ref.shape, F32)
        scores(0, 0)
        scores(1, 1)
        softmax(0)
        trips = (n_chunks - 2) // per_trip

        def body(j, carry):
            for r in range(per_trip):
                step(1 + per_trip * j + r, (1 + r) % n)
            return carry

        lax.fori_loop(0, trips, body, 0)
        for c in range(1 + per_trip * trips, n_chunks - 1):
            step(c, c % n)
        values(n_chunks - 2, (n_chunks - 2) % n)
        softmax((n_chunks - 1) % n)
        values(n_chunks - 1, (n_chunks - 1) % n)

    if frame is None:
        rescaling_pass()
        return

    excess_ref = c_bufs[0]

    def frame_values_g(c, slot, g):
        tv = vt_ref.shape[2]
        pv = None
        for j in range(tk // tv):
            part = jnp.dot(vt_ref[c * (tk // tv) + j], p_bufs[slot][j * tv:(j + 1) * tv, g],
                           preferred_element_type=F32)
            pv = part if pv is None else pv + part
        if frame.weighted:
            pv = pv * a_bufs[slot][:, g]
        acc_ref[:, g] = acc_ref[:, g] + pv

    def frame_scores_g(c, slot, g):
        qg, k_extra, shift, weight = frame.operands(c, g)
        k = k_ref[pl.ds(pl.multiple_of(c * tk, tk), tk), :]
        if k_extra is not None:
            k = jnp.concatenate([k, k_extra], axis=1)
        st = _nt_dot(k, qg)
        base = m_ref[:, g] if shift is None else m_ref[:, g] + shift
        cmax = psum = None
        for r in range(0, tk, FLASH_STRIP):
            strip = st[r:r + FLASH_STRIP, :]
            cmax = strip if cmax is None else jnp.maximum(cmax, strip)
            p = jnp.exp2(strip - base)
            psum = p if psum is None else psum + p
            p_bufs[slot][r:r + FLASH_STRIP, g] = p.astype(MXU_DTYPE)
        excess_ref[:, g] = jnp.maximum(excess_ref[:, g], jnp.max(cmax, axis=0, keepdims=True) - base)
        csum = jnp.sum(psum, axis=0, keepdims=True)
        if frame.weighted:
            a_bufs[slot][:, g] = weight
            csum = csum * weight
        l_ref[:, g] = l_ref[:, g] + csum

    def frame_step(j, slot):
        for g in groups:
            frame_values_g(frame.order(j - 1), 1 - slot, g)
            frame_scores_g(frame.order(j), slot, g)

    acc_ref[...] = jnp.zeros(acc_ref.shape, F32)
    excess_ref[...] = jnp.zeros(excess_ref.shape, F32)
    c0 = frame.order(0)
    for g in groups:
        k = k_ref[pl.ds(pl.multiple_of(c0 * tk, tk), tk), :]
        st = _nt_dot(k, q[g, :])
        if bias_fn is not None:
            st = bias_fn(st, c0)
        top = jnp.max(st, axis=0, keepdims=True)
        m_ref[:, g] = top
        p = jnp.exp2(st - top)
        l_ref[:, g] = jnp.sum(p, axis=0, keepdims=True)
        p_bufs[0][:, g] = p.astype(MXU_DTYPE)
        if frame.weighted:
            a_bufs[0][:, g] = jnp.ones((1, cols), F32)
    frame_trips = (n_chunks - 1) // per_trip

    def frame_body(t, carry):
        for r in range(per_trip):
            frame_step(1 + per_trip * t + r, (1 + r) % 2)
        return carry

    assert per_trip % 2 == 0
    lax.fori_loop(0, frame_trips, frame_body, 0)
    for j in range(1 + per_trip * frame_trips, n_chunks):
        frame_step(j, j % 2)
    for g in groups:
        frame_values_g(frame.order(n_chunks - 1), (n_chunks - 1) % 2, g)

    @pl.when(jnp.max(excess_ref[...]) > FRAME_HEADROOM)
    def _():
        rescaling_pass()


POS_SPLIT = 128
SLOPE_PIECES = 3


def _row_index_columns(tk, dtype):
    i = np.arange(tk)
    cols = np.zeros((tk, LANES), np.float32)
    for piece in range(SLOPE_PIECES):
        cols[:, 2 * piece] = i // POS_SPLIT
        cols[:, 2 * piece + 1] = i % POS_SPLIT
    return jnp.asarray(cols, dtype)


def _kv_chunk(seq):
    return min(KV_CHUNK, max(TOKEN_TILE, seq // 2))


def _da_attn_kernel(sc_ref, q_ref, k_ref, vt_ref, g_ref, pos_ref, o_ref, dist_ref, qa_ref, *scratch,
                    tq, tk, n_chunks, n_heads, out_scale):
    l_ref, acc_ref = scratch[4 * FLASH_SLOTS + 1:]
    h, qi = pl.program_id(1), pl.program_id(2)
    slope, lam = sc_ref[h], sc_ref[n_heads]
    q = q_ref[...]
    lane = lax.broadcasted_iota(jnp.int32, q.shape, 1)
    zero = jnp.zeros_like(q)
    d = q.shape[1] // 2
    qs = jnp.concatenate([jnp.where(lane < d, q, zero), jnp.where(lane >= d, q, zero)], axis=0)
    row = lax.broadcasted_iota(jnp.int32, (tk, tq), 0)
    col = lax.broadcasted_iota(jnp.int32, (tk, tq), 1)
    dist_ref[...] = (row - col - qi * tq).astype(F32)

    def bias_fn(st, c):
        b = slope * jnp.abs(dist_ref[...] + lax.convert_element_type(c * tk, F32))
        return jnp.concatenate([st[:, i:i + tq] - b for i in range(0, st.shape[1], tq)], axis=1)

    pieces = [sc_ref[n_heads + 1 + SLOPE_PIECES * h + i] for i in range(SLOPE_PIECES)]
    coef = jnp.zeros(qs.shape, F32)
    lane2 = lax.broadcasted_iota(jnp.int32, qs.shape, 1)
    for i, piece in enumerate(pieces):
        coef = jnp.where(lane2 == 2 * i, piece * float(POS_SPLIT), jnp.where(lane2 == 2 * i + 1, piece, coef))
    qa_ref[0] = jnp.concatenate([qs, (-coef).astype(qs.dtype)], axis=1)
    qa_ref[1] = jnp.concatenate([qs, coef.astype(qs.dtype)], axis=1)
    home = (qi * tq) // tk
    qpos = (qi * tq + lax.broadcasted_iota(jnp.int32, (1, tq), 1)).astype(F32)

    def operands(c, g):
        before = c < home
        start = lax.convert_element_type(c * tk, F32)
        shift = jnp.where(before, slope * (tk - 1), 0.0)
        edge = jnp.where(before, (start + (tk - 1)) - qpos, qpos - start)
        weight = jnp.exp2(slope * edge)
        return qa_ref[before.astype(jnp.int32)][g, :], pos_ref[...], shift, weight

    frame = _Frame(order=lambda j: lax.rem(home + j, jnp.int32(n_chunks)), operands=operands, weighted=True)
    _flash_t(qs, k_ref, vt_ref, n_chunks, tk, scratch, bias_fn, tq, frame)
    o = acc_ref[...] * (1.0 / l_ref[...])
    o = o[:, :tq] - lam * o[:, tq:]
    o = o * lax.rsqrt(jnp.mean(o * o, axis=0, keepdims=True) + DA_SUBLN_EPS) * g_ref[...] * out_scale
    o_ref[...] = o.T.astype(o_ref.dtype)


def _da_attn(scalars, q, k, vt, subln_col, batch, seq, row0, out_scale):
    dm = q.shape[1]
    tq, tk, tv = DA_Q_TILE, _kv_chunk(seq), vt.shape[2]
    hw = 2 * DA_HEAD_DIM
    n_heads = dm // hw
    assert row0 % seq == 0 and seq % tq == 0 and seq % tk == 0 and tk % tv == 0
    s0, q0, nq, n_chunks = row0 // seq, row0 // tq, seq // tq, seq // tk
    assert tk % tq == 0 and tk % POS_SPLIT == 0 and tk // POS_SPLIT <= 256
    pos = _row_index_columns(tk, q.dtype)
    return pl.pallas_call(
        functools.partial(_da_attn_kernel, tq=tq, tk=tk, n_chunks=n_chunks, n_heads=n_heads, out_scale=out_scale),
        grid_spec=pltpu.PrefetchScalarGridSpec(
            num_scalar_prefetch=1, grid=(batch, n_heads, nq),
            in_specs=[pl.BlockSpec((tq, hw), lambda b, h, i, sc: (q0 + b * nq + i, h)),
                      pl.BlockSpec((seq, hw), lambda b, h, i, sc: (s0 + b, h)),
                      pl.BlockSpec((seq // tv, hw, tv), lambda b, h, i, sc: (s0 + b, h, 0)),
                      pl.BlockSpec(subln_col.shape, lambda b, h, i, sc: (0, 0)),
                      pl.BlockSpec(pos.shape, lambda b, h, i, sc: (0, 0))],
            out_specs=pl.BlockSpec((tq, hw), lambda b, h, i, sc: (b * nq + i, h)),
            scratch_shapes=[pltpu.VMEM((tk, tq), F32), pltpu.VMEM((2, 2 * tq, hw + LANES), q.dtype)]
            + _flash_scratch(tk, 2 * tq, hw)),
        out_shape=jax.ShapeDtypeStruct((batch * seq, dm), MXU_DTYPE),
        compiler_params=_params(3), name="da_attn",
    )(scalars, q, k, vt, subln_col, pos)


def _mla_attn_kernel(qn_ref, qr_ref, kn_ref, kr_ref, vt_ref, o_ref, kfull_ref, *scratch, tk, n_chunks):
    l_ref, acc_ref = scratch[4 * FLASH_SLOTS + 1:]

    @pl.when(pl.program_id(2) == 0)
    def _():
        def copy(c, carry):
            rows = pl.ds(pl.multiple_of(c * tk, tk), tk)
            kfull_ref[rows, :LANES] = kn_ref[rows, :]
            kfull_ref[rows, LANES:] = kr_ref[rows, :]
            return carry
        lax.fori_loop(0, n_chunks, copy, 0)

    q = jnp.concatenate([qn_ref[...], qr_ref[...]], axis=1)
    frame = _Frame(order=lambda j: j, operands=lambda c, g: (q[g, :], None, None, None), weighted=False)
    _flash_t(q, kfull_ref, vt_ref, n_chunks, tk, scratch, None, MLA_COLS, frame)
    o = acc_ref[...] * (1.0 / l_ref[...])
    o_ref[...] = o.T.astype(o_ref.dtype)


def _mla_attn(q, kn, kr, vt, batch, seq, row0):
    n_heads = MLA_HEADS
    tq, tk, tv = MLA_Q_TILE, _kv_chunk(seq), vt.shape[2]
    assert row0 % seq == 0 and seq % tq == 0 and seq % tk == 0 and tk % tv == 0
    s0, q0, nq, n_chunks = row0 // seq, row0 // tq, seq // tq, seq // tk
    return pl.pallas_call(
        functools.partial(_mla_attn_kernel, tk=tk, n_chunks=n_chunks),
        grid=(batch, n_heads, nq),
        in_specs=[pl.BlockSpec((tq, LANES), lambda b, h, i: (q0 + b * nq + i, h)),
                  pl.BlockSpec((tq, LANES), lambda b, h, i: (q0 + b * nq + i, n_heads + h)),
                  pl.BlockSpec((seq, LANES), lambda b, h, i: (s0 + b, h)),
                  pl.BlockSpec((seq, LANES), lambda b, h, i: (s0 + b, 0)),
                  pl.BlockSpec((seq // tv, MLA_V, tv), lambda b, h, i: (s0 + b, h, 0))],
        out_specs=pl.BlockSpec((tq, MLA_V), lambda b, h, i: (b * nq + i, h)),
        out_shape=jax.ShapeDtypeStruct((batch * seq, n_heads * MLA_V), MXU_DTYPE),
        scratch_shapes=[pltpu.VMEM((seq, 2 * LANES), MXU_DTYPE)] + _flash_scratch(tk, tq, MLA_V),
        compiler_params=_params(3), name="mla_attn",
    )(q, q, kn, kr, vt)


R_E0, R_E1, R_RANK0, R_RANK1, R_GATE0, R_GATE1 = range(6)


def _router_kernel(x_ref, g_ref, w_ref, tri_ref, rec_ref, cnt_ref, base_ref):
    i = pl.program_id(0)

    @pl.when(i == 0)
    def _():
        base_ref[...] = jnp.zeros(base_ref.shape, F32)

    xn = _rms(x_ref[...], g_ref[...], NORM_EPS)
    logits = jnp.dot(xn, w_ref[...], preferred_element_type=F32, precision=lax.Precision.HIGHEST)
    tm = logits.shape[0]
    lane = lax.broadcasted_iota(jnp.int32, logits.shape, 1)
    neg = jnp.float32(-jnp.inf)

    def first_lane(mask):
        return jnp.min(jnp.where(mask, lane, LANES), axis=1, keepdims=True)

    is_g = lane < N_GROUPS
    lg = jnp.where(is_g, logits, neg)
    mg = jnp.max(lg, axis=1, keepdims=True)
    pg_sel = 1.0 / jnp.sum(jnp.exp(lg - mg), axis=1, keepdims=True)
    g_sel = first_lane(lg == mg)
    e_lo = N_GROUPS + EXPERTS_PER_GROUP * g_sel
    in_e = (lane >= e_lo) & (lane < e_lo + EXPERTS_PER_GROUP)
    le = jnp.where(in_e, logits, neg)
    me = jnp.max(le, axis=1, keepdims=True)
    ee = jnp.exp(le - me)
    pe = ee / jnp.sum(ee, axis=1, keepdims=True)
    pe = jnp.where(in_e, pe, -1.0)
    p0 = jnp.max(pe, axis=1, keepdims=True)
    i0 = first_lane(pe == p0)
    pe1 = jnp.where(lane == i0, -1.0, pe)
    p1 = jnp.max(pe1, axis=1, keepdims=True)
    i1 = first_lane(pe1 == p1)
    gate0 = pg_sel * p0 / (p0 + p1)
    gate1 = pg_sel * p1 / (p0 + p1)
    hit0, hit1 = lane == i0, lane == i1
    oh = jnp.concatenate([jnp.where(hit0, 1.0, 0.0), jnp.where(hit1, 1.0, 0.0)], axis=1)
    before = jnp.dot(tri_ref[...], oh.astype(MXU_DTYPE), preferred_element_type=F32)
    base = base_ref[...]
    tot0 = jnp.sum(oh[:, :LANES], axis=0, keepdims=True)
    tot1 = jnp.sum(oh[:, LANES:], axis=0, keepdims=True)
    rank0 = jnp.sum(jnp.where(hit0, before[:, :LANES] + base, 0.0), axis=1, keepdims=True)
    rank1 = jnp.sum(jnp.where(hit1, before[:, LANES:] + (base + tot0), 0.0), axis=1, keepdims=True)
    base = base + tot0 + tot1
    base_ref[...] = base
    cnt_ref[...] = base
    rec = jnp.zeros(logits.shape, F32)
    for pos, val in ((R_E0, (i0 - N_GROUPS).astype(F32)), (R_E1, (i1 - N_GROUPS).astype(F32)),
                     (R_RANK0, rank0), (R_RANK1, rank1), (R_GATE0, gate0), (R_GATE1, gate1)):
        rec = jnp.where(lane == pos, val, rec)
    rec_ref[...] = rec


def _router(x, g, w_router, tri):
    n, d = x.shape
    tm = TOKEN_TILE
    full = lambda a: pl.BlockSpec(a.shape, lambda i: (0,) * a.ndim)
    return pl.pallas_call(
        _router_kernel,
        grid=(n // tm,),
        in_specs=[pl.BlockSpec((tm, d), lambda i: (i, 0)), full(g), full(w_router), full(tri)],
        out_specs=[pl.BlockSpec((tm, LANES), lambda i: (i, 0)), pl.BlockSpec((1, LANES), lambda i: (0, 0))],
        out_shape=[jax.ShapeDtypeStruct((n, LANES), F32), jax.ShapeDtypeStruct((1, LANES), F32)],
        scratch_shapes=[pltpu.VMEM((1, LANES), F32)],
        compiler_params=_params(1), name="moe_router",
    )(x, g, w_router, tri)


ROW_TILE = 8


def _row_copy(src_ref, src_row, dst_ref, dst_row, sem):
    src = src_ref.at[pl.ds(pl.multiple_of(src_row * ROW_TILE, ROW_TILE), ROW_TILE), :]
    dst = dst_ref.at[pl.ds(pl.multiple_of(dst_row * ROW_TILE, ROW_TILE), ROW_TILE), :]
    return pltpu.make_async_copy(src, dst, sem)


def _to_row_tiles(ref, x):
    rows = x.shape[0]
    for c in range(ROW_TILE):
        ref[pl.ds(c, rows, stride=ROW_TILE), :] = x[:, c * LANES:(c + 1) * LANES]


def _from_row_tiles(ref):
    rows = ref.shape[0] // ROW_TILE
    return jnp.concatenate([ref[pl.ds(c, rows, stride=ROW_TILE), :] for c in range(ROW_TILE)], axis=1)


def _dispatch_kernel(dest_ref, x_ref, g_ref, xs_in_ref, xs_ref, xn_ref, sem, *, tm):
    del xs_in_ref
    _to_row_tiles(xn_ref, _rms(x_ref[...], g_ref[...], NORM_EPS))

    def issue(r, carry):
        for c in range(2):
            _row_copy(xn_ref, r, xs_ref, dest_ref[2 * r + c], sem).start()
        return carry
    lax.fori_loop(0, tm, issue, 0, unroll=8)

    def drain(r, carry):
        for c in range(2):
            _row_copy(xn_ref, 0, xs_ref, 0, sem).wait()
        return carry
    lax.fori_loop(0, tm, drain, 0, unroll=8)


def _dispatch(dest_flat, x, g, xs_zero):
    n, d = x.shape
    tm = TOKEN_TILE
    return pl.pallas_call(
        functools.partial(_dispatch_kernel, tm=tm),
        grid=(n // tm,),
        in_specs=[pl.BlockSpec((2 * tm,), lambda i: (i,), memory_space=pltpu.SMEM),
                  pl.BlockSpec((tm, d), lambda i: (i, 0)), pl.BlockSpec(g.shape, lambda i: (0, 0)),
                  pl.BlockSpec(memory_space=pl.ANY)],
        out_specs=pl.BlockSpec(memory_space=pl.ANY),
        out_shape=jax.ShapeDtypeStruct(xs_zero.shape, F32),
        scratch_shapes=[pltpu.VMEM((tm * ROW_TILE, LANES), F32), pltpu.SemaphoreType.DMA(())],
        input_output_aliases={3: 0},
        compiler_params=_params(1), name="moe_dispatch",
    )(dest_flat, x, g, xs_zero)


def _expert_kernel(be_ref, nv_ref, xs_ref, w1_ref, w3_ref, w2_ref, y_ref, w1c_ref, w3c_ref, w2c_ref):
    b = pl.program_id(0)

    @pl.when((b == 0) | (be_ref[b] != be_ref[jnp.maximum(b - 1, 0)]))
    def _():
        w1c_ref[...] = w1_ref[0, 0].astype(MXU_DTYPE)
        w3c_ref[...] = w3_ref[0, 0].astype(MXU_DTYPE)
        w2c_ref[...] = w2_ref[0, 0].astype(MXU_DTYPE)

    @pl.when(b < nv_ref[0])
    def _():
        xb = _from_row_tiles(xs_ref).astype(MXU_DTYPE)
        h1 = jnp.dot(xb, w1c_ref[...], preferred_element_type=F32)
        h3 = jnp.dot(xb, w3c_ref[...], preferred_element_type=F32)
        h = (h1 * jax.nn.sigmoid(h1) * h3).astype(MXU_DTYPE)
        _to_row_tiles(y_ref, jnp.dot(h, w2c_ref[...], preferred_element_type=F32))

    @pl.when(b >= nv_ref[0])
    def _():
        y_ref[...] = jnp.zeros(y_ref.shape, F32)


def _experts(block_expert, n_valid, xs, w1, w3, w2, layer):
    rb = EXPERT_ROWS
    d, de = w1.shape[2:]
    assert d == ROW_TILE * LANES
    return pl.pallas_call(
        _expert_kernel,
        grid_spec=pltpu.PrefetchScalarGridSpec(
            num_scalar_prefetch=2, grid=(xs.shape[0] // (rb * ROW_TILE),),
            in_specs=[pl.BlockSpec((rb * ROW_TILE, LANES), lambda b, be, nv: (b, 0)),
                      pl.BlockSpec((1, 1, d, de), lambda b, be, nv: (layer, be[b], 0, 0)),
                      pl.BlockSpec((1, 1, d, de), lambda b, be, nv: (layer, be[b], 0, 0)),
                      pl.BlockSpec((1, 1, de, d), lambda b, be, nv: (layer, be[b], 0, 0))],
            out_specs=pl.BlockSpec((rb * ROW_TILE, LANES), lambda b, be, nv: (b, 0)),
            scratch_shapes=[pltpu.VMEM((d, de), MXU_DTYPE), pltpu.VMEM((d, de), MXU_DTYPE),
                            pltpu.VMEM((de, d), MXU_DTYPE)]),
        out_shape=jax.ShapeDtypeStruct(xs.shape, F32),
        compiler_params=_params(1), name="moe_experts",
    )(block_expert, n_valid, xs, w1, w3, w2)


def _combine_kernel(dest_ref, x_ref, rec_ref, gfin_ref, yb_ref, o_ref, ybuf_ref, sem, *, tm, final):
    def issue(r, carry):
        for c in range(2):
            _row_copy(yb_ref, dest_ref[2 * r + c], ybuf_ref.at[c], r, sem).start()
        return carry
    lax.fori_loop(0, tm, issue, 0, unroll=8)

    def drain(r, carry):
        for c in range(2):
            _row_copy(yb_ref, 0, ybuf_ref.at[c], 0, sem).wait()
        return carry
    lax.fori_loop(0, tm, drain, 0, unroll=8)

    rec = rec_ref[...]
    out = (x_ref[...] + rec[:, R_GATE0:R_GATE0 + 1] * _from_row_tiles(ybuf_ref.at[0])
           + rec[:, R_GATE1:R_GATE1 + 1] * _from_row_tiles(ybuf_ref.at[1]))
    if final:
        out = _rms(out, gfin_ref[...], NORM_EPS)
    o_ref[...] = out


def _combine(dest_flat, x, rec, g_final, yb, final):
    n, d = x.shape
    tm = TOKEN_TILE
    return pl.pallas_call(
        functools.partial(_combine_kernel, tm=tm, final=final),
        grid=(n // tm,),
        in_specs=[pl.BlockSpec((2 * tm,), lambda i: (i,), memory_space=pltpu.SMEM),
                  pl.BlockSpec((tm, d), lambda i: (i, 0)), pl.BlockSpec((tm, LANES), lambda i: (i, 0)),
                  pl.BlockSpec(g_final.shape, lambda i: (0, 0)), pl.BlockSpec(memory_space=pl.ANY)],
        out_specs=pl.BlockSpec((tm, d), lambda i: (i, 0)),
        out_shape=jax.ShapeDtypeStruct((n, d), F32),
        scratch_shapes=[pltpu.VMEM((2, tm * ROW_TILE, LANES), F32), pltpu.SemaphoreType.DMA(())],
        compiler_params=_params(1), name="moe_combine",
    )(dest_flat, x, rec, g_final, yb)


def _moe(x, g, w_router, tri, w1, w3, w2, layer, g_final, final):
    n, d = x.shape
    rb = EXPERT_ROWS
    rec, counts = _router(x, g, w_router, tri)
    counts = counts[0, N_GROUPS:N_GROUPS + N_EXPERTS].astype(jnp.int32)
    padded = (counts + rb - 1) // rb * rb
    pad_end = jnp.cumsum(padded)
    pad_start = pad_end - padded
    n_blocks = (2 * n + N_EXPERTS * (rb - 1)) // rb
    expert = rec[:, R_E0:R_E1 + 1].astype(jnp.int32)
    rank = rec[:, R_RANK0:R_RANK1 + 1].astype(jnp.int32)
    before = expert[..., None] > jnp.arange(N_EXPERTS, dtype=jnp.int32)
    dest = (jnp.sum(jnp.where(before, padded, 0), axis=-1) + rank).reshape(-1)
    block_start = jnp.arange(n_blocks, dtype=jnp.int32) * rb
    block_expert = jnp.minimum(jnp.searchsorted(pad_end, block_start, side="right"), N_EXPERTS - 1).astype(jnp.int32)
    n_valid = (pad_end[-1:] // rb).astype(jnp.int32)
    xs = _dispatch(dest, x, g, jnp.zeros((n_blocks * rb * ROW_TILE, LANES), F32))
    yb = _experts(block_expert, n_valid, xs, w1, w3, w2, layer)
    return _combine(dest, x, rec, g_final, yb, final)


def _rope_tables(positions):
    inv = 1.0 / (ROPE_THETA ** (jnp.arange(0, MLA_ROPE, 2, dtype=F32) / MLA_ROPE))
    ang = positions.astype(F32)[:, None] * inv[None, :]
    cos = jnp.repeat(jnp.cos(ang), 2, axis=1)
    sin = jnp.stack([-jnp.sin(ang), jnp.sin(ang)], axis=-1).reshape(ang.shape[0], MLA_ROPE)
    pad = ((0, 0), (0, LANES - MLA_ROPE))
    return jnp.pad(cos, pad), jnp.pad(sin, pad)


def _pair_swap(w):
    return w.reshape(w.shape[0], -1, 2)[:, :, ::-1].reshape(w.shape)


def kernel(x_prompt, x_sample, ln_mix, ln_ffn, ln_final, da_wqkv, da_wo, da_lambda_q1, da_lambda_k1, da_lambda_q2, da_lambda_k2, da_subln, mla_w_down, mla_q_norm, mla_w_uq, mla_kv_norm, mla_w_ukv, mla_wo, moe_w_group, moe_w_expert, moe_w1, moe_w3, moe_w2):
    bp, sp, d = x_prompt.shape
    bs, ss, _ = x_sample.shape
    n_p, n_s = bp * sp, bs * ss
    groups = ((bp, sp, 0), (bs, ss, n_p))
    x = jnp.concatenate([x_prompt.reshape(n_p, d), x_sample.reshape(n_s, d)], axis=0)
    depth = ln_mix.shape[0]
    cdt = MXU_DTYPE

    positions = jnp.concatenate([jnp.tile(jnp.arange(sp), bp), jnp.tile(jnp.arange(ss), bs)])
    cos, sin = _rope_tables(positions)
    tri = jnp.tril(jnp.ones((TOKEN_TILE, TOKEN_TILE), F32), -1).astype(cdt)
    n_da_heads = d // (2 * DA_HEAD_DIM)
    slopes = 2.0 ** (-8.0 * np.arange(1, n_da_heads + 1) / n_da_heads) * LOG2E
    hd = MLA_HEADS * LANES
    zpad = LANES - MLA_ROPE
    rest, parts = jnp.asarray(slopes, F32), []
    for _ in range(SLOPE_PIECES):
        parts.append(rest.astype(jnp.bfloat16).astype(F32))
        rest = rest - parts[-1]
    slope_pieces = jnp.stack(parts, axis=1).reshape(-1)

    for i in range(depth):
        j = i // 2
        g_mix = ln_mix[i][None, :]
        if i % 2 == 0:
            lambda_init = 0.8 - 0.6 * math.exp(-0.3 * i)
            lam = (jnp.exp(jnp.sum(da_lambda_q1[j] * da_lambda_k1[j])) - jnp.exp(jnp.sum(da_lambda_q2[j] * da_lambda_k2[j]))
                   + lambda_init)
            scalars = jnp.concatenate([jnp.asarray(slopes, F32), lam[None].astype(F32), slope_pieces])
            w = da_wqkv[j]
            q, k, vt = _da_qkv(x, g_mix, w[:, :d].astype(cdt), w[:, d:2 * d].astype(cdt), w[:, 2 * d:].T.astype(cdt),
                               DA_HEAD_DIM ** -0.5 * LOG2E)
            oa, ob = [_da_attn(scalars, q, k, vt, da_subln[j][:, None], batch, seq, row0, 1.0 - lambda_init)
                      for batch, seq, row0 in groups]
            x = _out_proj(x, oa, ob, da_wo[j].astype(cdt))
        else:
            qr, kvr = MLA_Q_RANK, MLA_KV_RANK
            wd = mla_w_down[j]
            w_rope = wd[:, qr + kvr:]
            wd = jnp.concatenate([wd[:, :qr + kvr], jnp.pad(w_rope, ((0, 0), (0, zpad))),
                                  jnp.pad(_pair_swap(w_rope), ((0, 0), (0, zpad)))], axis=1)
            wuq = mla_w_uq[j].reshape(qr, MLA_HEADS, MLA_NOPE + MLA_ROPE)
            wq_rope = wuq[:, :, MLA_NOPE:]
            pad3 = ((0, 0), (0, 0), (0, zpad))
            wuq = jnp.concatenate([wuq[:, :, :MLA_NOPE].reshape(qr, hd), jnp.pad(wq_rope, pad3).reshape(qr, hd),
                                   jnp.pad(_pair_swap(wq_rope.reshape(qr, -1)).reshape(wq_rope.shape), pad3).reshape(qr, hd)],
                                  axis=1)
            wukv = mla_w_ukv[j].reshape(kvr, MLA_HEADS, MLA_NOPE + MLA_V)
            wkn = wukv[:, :, :MLA_NOPE].reshape(kvr, hd)
            wvt = wukv[:, :, MLA_NOPE:].reshape(kvr, MLA_HEADS * MLA_V).T
            q, kn, kr, vt = _mla_proj(x, g_mix, wd.astype(cdt), mla_q_norm[j][None, :], mla_kv_norm[j][None, :],
                                      wuq.astype(cdt), wkn.astype(cdt), wvt.astype(cdt), cos, sin,
                                      (MLA_NOPE + MLA_ROPE) ** -0.5 * LOG2E)
            oa, ob = [_mla_attn(q, kn, kr, vt, batch, seq, row0) for batch, seq, row0 in groups]
            x = _out_proj(x, oa, ob, mla_wo[j].astype(cdt))
        w_router = jnp.pad(jnp.concatenate([moe_w_group[i], moe_w_expert[i]], axis=1),
                           ((0, 0), (0, LANES - N_GROUPS - N_EXPERTS)))
        x = _moe(x, ln_ffn[i][None, :], w_router, tri, moe_w1, moe_w3, moe_w2, i, ln_final[None, :],
                 final=(i == depth - 1))
    return x[:n_p].reshape(bp, sp, d), x[n_p:].reshape(bs, ss, d)
```

```python
import functools
import math

import numpy as np
import jax
import jax.numpy as jnp
from jax import lax
from jax.experimental import pallas as pl
from jax.experimental.pallas import tpu as pltpu

F32 = jnp.float32
MXU_DTYPE = jnp.bfloat16
LOG2E = 1.4426950408889634

NORM_EPS = 1e-6
DA_SUBLN_EPS = 1e-5
ROPE_THETA = 10000.0
DA_HEAD_DIM = 64
MLA_HEADS = 8
MLA_NOPE = 128
MLA_ROPE = 64
MLA_V = 128
MLA_Q_RANK = 384
MLA_KV_RANK = 256
N_GROUPS = 8
EXPERTS_PER_GROUP = 8
N_EXPERTS = N_GROUPS * EXPERTS_PER_GROUP

LANES = 128
TOKEN_TILE = 512
KV_CHUNK = 512
DA_Q_TILE = 512
DA_COLS = 256
MLA_Q_TILE = 512
MLA_COLS = 256
EXPERT_ROWS = 256
VMEM_LIMIT = 56 * 1024 * 1024


def _rms(x, g, eps):
    return x * lax.rsqrt(jnp.mean(x * x, axis=-1, keepdims=True) + eps) * g


def _nt_dot(a, b):
    return lax.dot_general(a, b, (((1,), (1,)), ((), ())), preferred_element_type=F32)


def _params(n_axes):
    return pltpu.CompilerParams(dimension_semantics=("arbitrary",) * n_axes, vmem_limit_bytes=VMEM_LIMIT)


def _da_qkv_kernel(x_ref, g_ref, wq_ref, wk_ref, wvt_ref, q_ref, k_ref, vt_ref, *, q_scale):
    xn = _rms(x_ref[...], g_ref[...], NORM_EPS).astype(MXU_DTYPE)
    q_ref[...] = (jnp.dot(xn, wq_ref[...], preferred_element_type=F32) * q_scale).astype(q_ref.dtype)
    k_ref[...] = jnp.dot(xn, wk_ref[...], preferred_element_type=F32).astype(k_ref.dtype)
    vt_ref[0] = _nt_dot(wvt_ref[...], xn).astype(vt_ref.dtype)


def _da_qkv(x, g, wq, wk, wvt, q_scale):
    n, d = x.shape
    tm = TOKEN_TILE
    full = lambda a: pl.BlockSpec(a.shape, lambda i: (0,) * a.ndim)
    return pl.pallas_call(
        functools.partial(_da_qkv_kernel, q_scale=q_scale),
        grid=(n // tm,),
        in_specs=[pl.BlockSpec((tm, d), lambda i: (i, 0)), full(g), full(wq), full(wk), full(wvt)],
        out_specs=[pl.BlockSpec((tm, d), lambda i: (i, 0)), pl.BlockSpec((tm, d), lambda i: (i, 0)),
                   pl.BlockSpec((1, d, tm), lambda i: (i, 0, 0))],
        out_shape=[jax.ShapeDtypeStruct((n, d), MXU_DTYPE), jax.ShapeDtypeStruct((n, d), MXU_DTYPE),
                   jax.ShapeDtypeStruct((n // tm, d, tm), MXU_DTYPE)],
        compiler_params=_params(1), name="da_qkv",
    )(x, g, wq, wk, wvt)


def _mla_proj_kernel(x_ref, g_ref, wd_ref, qg_ref, kvg_ref, wuq_ref, wkn_ref, wvt_ref, cos_ref, sin_ref,
                     q_ref, kn_ref, kr_ref, vt_ref, *, q_scale):
    xn = _rms(x_ref[...], g_ref[...], NORM_EPS).astype(MXU_DTYPE)
    down = jnp.dot(xn, wd_ref[...], preferred_element_type=F32)
    qr, kvr = MLA_Q_RANK, MLA_KV_RANK
    c_q = _rms(down[:, :qr], qg_ref[...], NORM_EPS).astype(MXU_DTYPE)
    c_kv = _rms(down[:, qr:qr + kvr], kvg_ref[...], NORM_EPS).astype(MXU_DTYPE)
    cos, sin = cos_ref[...], sin_ref[...]
    kr = down[:, qr + kvr:qr + kvr + LANES] * cos + down[:, qr + kvr + LANES:] * sin
    kr_ref[...] = kr.astype(kr_ref.dtype)
    hd = MLA_HEADS * LANES
    q = jnp.dot(c_q, wuq_ref[...], preferred_element_type=F32)
    q_ref[:, :hd] = (q[:, :hd] * q_scale).astype(q_ref.dtype)
    for h in range(MLA_HEADS):
        a = q[:, hd + h * LANES:hd + (h + 1) * LANES]
        b = q[:, 2 * hd + h * LANES:2 * hd + (h + 1) * LANES]
        q_ref[:, hd + h * LANES:hd + (h + 1) * LANES] = ((a * cos + b * sin) * q_scale).astype(q_ref.dtype)
    kn_ref[...] = jnp.dot(c_kv, wkn_ref[...], preferred_element_type=F32).astype(kn_ref.dtype)
    vt_ref[0] = _nt_dot(wvt_ref[...], c_kv).astype(vt_ref.dtype)


def _mla_proj(x, g, wd, qg, kvg, wuq, wkn, wvt, cos, sin, q_scale):
    n, d = x.shape
    tm = TOKEN_TILE
    hd = MLA_HEADS * LANES
    full = lambda a: pl.BlockSpec(a.shape, lambda i: (0,) * a.ndim)
    row = lambda w: pl.BlockSpec((tm, w), lambda i: (i, 0))
    return pl.pallas_call(
        functools.partial(_mla_proj_kernel, q_scale=q_scale),
        grid=(n // tm,),
        in_specs=[row(d), full(g), full(wd), full(qg), full(kvg), full(wuq), full(wkn), full(wvt),
                  row(LANES), row(LANES)],
        out_specs=[row(2 * hd), row(hd), row(LANES), pl.BlockSpec((1, hd, tm), lambda i: (i, 0, 0))],
        out_shape=[jax.ShapeDtypeStruct((n, 2 * hd), MXU_DTYPE), jax.ShapeDtypeStruct((n, hd), MXU_DTYPE),
                   jax.ShapeDtypeStruct((n, LANES), MXU_DTYPE),
                   jax.ShapeDtypeStruct((n // tm, hd, tm), MXU_DTYPE)],
        compiler_params=_params(1), name="mla_proj",
    )(x, g, wd, qg, kvg, wuq, wkn, wvt, cos, sin)


def _out_proj_kernel(x_ref, oa_ref, ob_ref, w_ref, y_ref, *, a_tiles):
    @pl.when(pl.program_id(0) < a_tiles)
    def _():
        y_ref[...] = x_ref[...] + jnp.dot(oa_ref[...], w_ref[...], preferred_element_type=F32)

    @pl.when(pl.program_id(0) >= a_tiles)
    def _():
        y_ref[...] = x_ref[...] + jnp.dot(ob_ref[...], w_ref[...], preferred_element_type=F32)


def _out_proj(x, oa, ob, w):
    n, d = x.shape
    tm = TOKEN_TILE
    a_tiles = oa.shape[0] // tm
    assert oa.shape[0] % tm == 0 and ob.shape[0] % tm == 0 and oa.shape[0] + ob.shape[0] == n
    return pl.pallas_call(
        functools.partial(_out_proj_kernel, a_tiles=a_tiles),
        grid=(n // tm,),
        in_specs=[pl.BlockSpec((tm, d), lambda i: (i, 0)),
                  pl.BlockSpec((tm, oa.shape[1]), lambda i: (jnp.minimum(i, a_tiles - 1), 0)),
                  pl.BlockSpec((tm, ob.shape[1]), lambda i: (jnp.maximum(i - a_tiles, 0), 0)),
                  pl.BlockSpec(w.shape, lambda i: (0, 0))],
        out_specs=pl.BlockSpec((tm, d), lambda i: (i, 0)),
        out_shape=jax.ShapeDtypeStruct((n, d), F32),
        compiler_params=_params(1), name="out_proj",
    )(x, oa, ob, w)


FLASH_STRIP = 32
FRAME_HEADROOM = 60.0
FLASH_ROUNDS = 2
FLASH_SLOTS = 3


def _flash_scratch(tk, w, dv):
    n = FLASH_SLOTS
    return ([pltpu.VMEM((tk, w), F32)] * n + [pltpu.VMEM((tk, w), MXU_DTYPE)] * n + [pltpu.VMEM((1, w), F32)] * (2 * n)
            + [pltpu.VMEM((1, w), F32), pltpu.VMEM((1, w), F32), pltpu.VMEM((dv, w), F32)])


class _Frame:
    def __init__(self, order, operands, weighted):
        self.order, self.operands, self.weighted = order, operands, weighted


def _flash_t(q, k_ref, vt_ref, n_chunks, tk, scratch, bias_fn, cols, frame):
    n = FLASH_SLOTS
    s_bufs, p_bufs, c_bufs, a_bufs = (scratch[i * n:(i + 1) * n] for i in range(4))
    m_ref, l_ref, acc_ref = scratch[4 * n:4 * n + 3]
    assert n_chunks >= 2
    w = q.shape[0]
    groups = [slice(g, g + cols) for g in range(0, w, cols)]

    def scores_g(c, slot, g):
        k = k_ref[pl.ds(pl.multiple_of(c * tk, tk), tk), :]
        st = _nt_dot(k, q[g, :])
        if bias_fn is not None:
            st = bias_fn(st, c, g)
        cmax = None
        for r in range(0, tk, FLASH_STRIP):
            strip = st[r:r + FLASH_STRIP, :]
            s_bufs[slot][r:r + FLASH_STRIP, g] = strip
            cmax = strip if cmax is None else jnp.maximum(cmax, strip)
        c_bufs[slot][:, g] = jnp.max(cmax, axis=0, keepdims=True)

    def softmax_g(slot, g):
        m_prev = m_ref[:, g]
        m_new = jnp.maximum(m_prev, c_bufs[slot][:, g])
        alpha = jnp.exp2(m_prev - m_new)
        m_ref[:, g] = m_new
        a_bufs[slot][:, g] = alpha
        psum = None
        for r in range(0, tk, FLASH_STRIP):
            p = jnp.exp2(s_bufs[slot][r:r + FLASH_STRIP, g] - m_new)
            psum = p if psum is None else psum + p
            p_bufs[slot][r:r + FLASH_STRIP, g] = p.astype(MXU_DTYPE)
        l_ref[:, g] = alpha * l_ref[:, g] + jnp.sum(psum, axis=0, keepdims=True)

    def values_g(c, slot, g):
        tv = vt_ref.shape[2]
        pv = None
        for j in range(tk // tv):
            part = jnp.dot(vt_ref[c * (tk // tv) + j], p_bufs[slot][j * tv:(j + 1) * tv, g],
                           preferred_element_type=F32)
            pv = part if pv is None else pv + part
        acc_ref[:, g] = acc_ref[:, g] * a_bufs[slot][:, g] + pv

    def scores(c, slot):
        for g in groups:
            scores_g(c, slot, g)

    def softmax(slot):
        for g in groups:
            softmax_g(slot, g)

    def values(c, slot):
        for g in groups:
            values_g(c, slot, g)

    def step(c, slot):
        for g in groups:
            values_g(c - 1, (slot - 1) % n, g)
            scores_g(c + 1, (slot + 1) % n, g)
            softmax_g(slot, g)

    per_trip = n * FLASH_ROUNDS

    def rescaling_pass():
        m_ref[...] = jnp.full(m_ref.shape, -jnp.inf, F32)
        l_ref[...] = jnp.zeros(l_ref.shape, F32)
        acc_ref[...] = jnp.zeros(acc_ref.shape, F32)
        scores(0, 0)
        scores(1, 1)
        softmax(0)
        trips = (n_chunks - 2) // per_trip

        def body(j, carry):
            for r in range(per_trip):
                step(1 + per_trip * j + r, (1 + r) % n)
            return carry

        lax.fori_loop(0, trips, body, 0)
        for c in range(1 + per_trip * trips, n_chunks - 1):
            step(c, c % n)
        values(n_chunks - 2, (n_chunks - 2) % n)
        softmax((n_chunks - 1) % n)
        values(n_chunks - 1, (n_chunks - 1) % n)

    if frame is None:
        rescaling_pass()
        return

    excess_ref = c_bufs[0]

    def frame_values_g(c, slot, g):
        tv = vt_ref.shape[2]
        pv = None
        for j in range(tk // tv):
            part = jnp.dot(vt_ref[c * (tk // tv) + j], p_bufs[slot][j * tv:(j + 1) * tv, g],
                           preferred_element_type=F32)
            pv = part if pv is None else pv + part
        if frame.weighted:
            pv = pv * a_bufs[slot][:, g]
        acc_ref[:, g] = acc_ref[:, g] + pv

    def frame_scores_g(c, slot, g):
        qg, k_extra, shift, weight = frame.operands(c, g)
        k = k_ref[pl.ds(pl.multiple_of(c * tk, tk), tk), :]
        if k_extra is not None:
            k = jnp.concatenate([k, k_extra], axis=1)
        st = _nt_dot(k, qg)
        base = m_ref[:, g] if shift is None else m_ref[:, g] + shift
        cmax = psum = None
        for r in range(0, tk, FLASH_STRIP):
            strip = st[r:r + FLASH_STRIP, :]
            cmax = strip if cmax is None else jnp.maximum(cmax, strip)
            p = jnp.exp2(strip - base)
            psum = p if psum is None else psum + p
            p_bufs[slot][r:r + FLASH_STRIP, g] = p.astype(MXU_DTYPE)
        excess_ref[:, g] = jnp.maximum(excess_ref[:, g], jnp.max(cmax, axis=0, keepdims=True) - base)
        csum = jnp.sum(psum, axis=0, keepdims=True)
        if frame.weighted:
            a_bufs[slot][:, g] = weight
            csum = csum * weight
        l_ref[:, g] = l_ref[:, g] + csum

    def frame_step(j, slot):
        for g in groups:
            frame_values_g(frame.order(j - 1), 1 - slot, g)
            frame_scores_g(frame.order(j), slot, g)

    acc_ref[...] = jnp.zeros(acc_ref.shape, F32)
    excess_ref[...] = jnp.zeros(excess_ref.shape, F32)
    c0 = frame.order(0)
    for g in groups:
        k = k_ref[pl.ds(pl.multiple_of(c0 * tk, tk), tk), :]
        st = _nt_dot(k, q[g, :])
        if bias_fn is not None:
            st = bias_fn(st, c0, g)
        top = jnp.max(st, axis=0, keepdims=True)
        m_ref[:, g] = top
        p = jnp.exp2(st - top)
        l_ref[:, g] = jnp.sum(p, axis=0, keepdims=True)
        p_bufs[0][:, g] = p.astype(MXU_DTYPE)
        if frame.weighted:
            a_bufs[0][:, g] = jnp.ones((1, cols), F32)
    frame_trips = (n_chunks - 1) // per_trip

    def frame_body(t, carry):
        for r in range(per_trip):
            frame_step(1 + per_trip * t + r, (1 + r) % 2)
        return carry

    assert per_trip % 2 == 0
    lax.fori_loop(0, frame_trips, frame_body, 0)
    for j in range(1 + per_trip * frame_trips, n_chunks):
        frame_step(j, j % 2)
    for g in groups:
        frame_values_g(frame.order(n_chunks - 1), (n_chunks - 1) % 2, g)

    @pl.when(jnp.max(excess_ref[...]) > FRAME_HEADROOM)
    def _():
        rescaling_pass()


POS_SPLIT = 128
SLOPE_PIECES = 3


def _row_index_columns(tk, dtype):
    i = np.arange(tk)
    cols = np.zeros((tk, LANES), np.float32)
    for piece in range(SLOPE_PIECES):
        cols[:, 2 * piece] = i // POS_SPLIT
        cols[:, 2 * piece + 1] = i % POS_SPLIT
    return jnp.asarray(cols, dtype)


def _kv_chunk(seq):
    return min(KV_CHUNK, max(TOKEN_TILE, seq // 2))


def _da_attn_kernel(sc_ref, q_ref, k_ref, vt_ref, g_ref, pos_ref, o_ref, dist_ref, qa_ref, *scratch,
                    tq, tk, n_chunks, n_heads, out_scale):
    l_ref, acc_ref = scratch[4 * FLASH_SLOTS + 1:]
    h, qi = pl.program_id(1), pl.program_id(2)
    slope, lam = sc_ref[h], sc_ref[n_heads]
    q = q_ref[...]
    lane = lax.broadcasted_iota(jnp.int32, q.shape, 1)
    zero = jnp.zeros_like(q)
    d = q.shape[1] // 2
    qs = jnp.concatenate([jnp.where(lane < d, q, zero), jnp.where(lane >= d, q, zero)], axis=0)
    row = lax.broadcasted_iota(jnp.int32, (tk, tq), 0)
    col = lax.broadcasted_iota(jnp.int32, (tk, tq), 1)
    dist_ref[...] = (row - col - qi * tq).astype(F32)

    assert tq % DA_COLS == 0

    def bias_fn(st, c, g):
        off = g.start % tq
        dist = dist_ref[:, off:off + DA_COLS] + lax.convert_element_type(c * tk, F32)
        return st - slope * jnp.abs(dist)

    pieces = [sc_ref[n_heads + 1 + SLOPE_PIECES * h + i] for i in range(SLOPE_PIECES)]
    coef = jnp.zeros(qs.shape, F32)
    lane2 = lax.broadcasted_iota(jnp.int32, qs.shape, 1)
    for i, piece in enumerate(pieces):
        coef = jnp.where(lane2 == 2 * i, piece * float(POS_SPLIT), jnp.where(lane2 == 2 * i + 1, piece, coef))
    qa_ref[0] = jnp.concatenate([qs, (-coef).astype(qs.dtype)], axis=1)
    qa_ref[1] = jnp.concatenate([qs, coef.astype(qs.dtype)], axis=1)
    home = (qi * tq) // tk
    qpos = (qi * tq + lax.broadcasted_iota(jnp.int32, (1, tq), 1)).astype(F32)

    def operands(c, g):
        before = c < home
        start = lax.convert_element_type(c * tk, F32)
        shift = jnp.where(before, slope * (tk - 1), 0.0)
        edge = jnp.where(before, (start + (tk - 1)) - qpos, qpos - start)
        weight = jnp.exp2(slope * edge)
        weight = weight[:, g.start % tq:g.start % tq + DA_COLS]
        return qa_ref[before.astype(jnp.int32)][g, :], pos_ref[...], shift, weight

    frame = _Frame(order=lambda j: lax.rem(home + j, jnp.int32(n_chunks)), operands=operands, weighted=True)
    _flash_t(qs, k_ref, vt_ref, n_chunks, tk, scratch, bias_fn, DA_COLS, frame)
    o = acc_ref[...] * (1.0 / l_ref[...])
    o = o[:, :tq] - lam * o[:, tq:]
    o = o * lax.rsqrt(jnp.mean(o * o, axis=0, keepdims=True) + DA_SUBLN_EPS) * g_ref[...] * out_scale
    o_ref[...] = o.T.astype(o_ref.dtype)


def _da_attn(scalars, q, k, vt, subln_col, batch, seq, row0, out_scale):
    dm = q.shape[1]
    tq, tk, tv = DA_Q_TILE, _kv_chunk(seq), vt.shape[2]
    hw = 2 * DA_HEAD_DIM
    n_heads = dm // hw
    assert row0 % seq == 0 and seq % tq == 0 and seq % tk == 0 and tk % tv == 0
    s0, q0, nq, n_chunks = row0 // seq, row0 // tq, seq // tq, seq // tk
    assert tk % tq == 0 and tk % POS_SPLIT == 0 and tk // POS_SPLIT <= 256
    pos = _row_index_columns(tk, q.dtype)
    return pl.pallas_call(
        functools.partial(_da_attn_kernel, tq=tq, tk=tk, n_chunks=n_chunks, n_heads=n_heads, out_scale=out_scale),
        grid_spec=pltpu.PrefetchScalarGridSpec(
            num_scalar_prefetch=1, grid=(batch, n_heads, nq),
            in_specs=[pl.BlockSpec((tq, hw), lambda b, h, i, sc: (q0 + b * nq + i, h)),
                      pl.BlockSpec((seq, hw), lambda b, h, i, sc: (s0 + b, h)),
                      pl.BlockSpec((seq // tv, hw, tv), lambda b, h, i, sc: (s0 + b, h, 0)),
                      pl.BlockSpec(subln_col.shape, lambda b, h, i, sc: (0, 0)),
                      pl.BlockSpec(pos.shape, lambda b, h, i, sc: (0, 0))],
            out_specs=pl.BlockSpec((tq, hw), lambda b, h, i, sc: (b * nq + i, h)),
            scratch_shapes=[pltpu.VMEM((tk, tq), F32), pltpu.VMEM((2, 2 * tq, hw + LANES), q.dtype)]
            + _flash_scratch(tk, 2 * tq, hw)),
        out_shape=jax.ShapeDtypeStruct((batch * seq, dm), MXU_DTYPE),
        compiler_params=_params(3), name="da_attn",
    )(scalars, q, k, vt, subln_col, pos)


def _mla_attn_kernel(qn_ref, qr_ref, kn_ref, kr_ref, vt_ref, o_ref, kfull_ref, *scratch, tk, n_chunks):
    l_ref, acc_ref = scratch[4 * FLASH_SLOTS + 1:]

    @pl.when(pl.program_id(2) == 0)
    def _():
        def copy(c, carry):
            rows = pl.ds(pl.multiple_of(c * tk, tk), tk)
            kfull_ref[rows, :LANES] = kn_ref[rows, :]
            kfull_ref[rows, LANES:] = kr_ref[rows, :]
            return carry
        lax.fori_loop(0, n_chunks, copy, 0)

    q = jnp.concatenate([qn_ref[...], qr_ref[...]], axis=1)
    frame = _Frame(order=lambda j: j, operands=lambda c, g: (q[g, :], None, None, None), weighted=False)
    _flash_t(q, kfull_ref, vt_ref, n_chunks, tk, scratch, None, MLA_COLS, frame)
    o = acc_ref[...] * (1.0 / l_ref[...])
    o_ref[...] = o.T.astype(o_ref.dtype)


def _mla_attn(q, kn, kr, vt, batch, seq, row0):
    n_heads = MLA_HEADS
    tq, tk, tv = MLA_Q_TILE, _kv_chunk(seq), vt.shape[2]
    assert row0 % seq == 0 and seq % tq == 0 and seq % tk == 0 and tk % tv == 0
    s0, q0, nq, n_chunks = row0 // seq, row0 // tq, seq // tq, seq // tk
    return pl.pallas_call(
        functools.partial(_mla_attn_kernel, tk=tk, n_chunks=n_chunks),
        grid=(batch, n_heads, nq),
        in_specs=[pl.BlockSpec((tq, LANES), lambda b, h, i: (q0 + b * nq + i, h)),
                  pl.BlockSpec((tq, LANES), lambda b, h, i: (q0 + b * nq + i, n_heads + h)),
                  pl.BlockSpec((seq, LANES), lambda b, h, i: (s0 + b, h)),
                  pl.BlockSpec((seq, LANES), lambda b, h, i: (s0 + b, 0)),
                  pl.BlockSpec((seq // tv, MLA_V, tv), lambda b, h, i: (s0 + b, h, 0))],
        out_specs=pl.BlockSpec((tq, MLA_V), lambda b, h, i: (b * nq + i, h)),
        out_shape=jax.ShapeDtypeStruct((batch * seq, n_heads * MLA_V), MXU_DTYPE),
        scratch_shapes=[pltpu.VMEM((seq, 2 * LANES), MXU_DTYPE)] + _flash_scratch(tk, tq, MLA_V),
        compiler_params=_params(3), name="mla_attn",
    )(q, q, kn, kr, vt)


R_E0, R_E1, R_RANK0, R_RANK1, R_GATE0, R_GATE1 = range(6)


def _router_kernel(x_ref, g_ref, w_ref, tri_ref, rec_ref, cnt_ref, base_ref):
    i = pl.program_id(0)

    @pl.when(i == 0)
    def _():
        base_ref[...] = jnp.zeros(base_ref.shape, F32)

    xn = _rms(x_ref[...], g_ref[...], NORM_EPS)
    logits = jnp.dot(xn, w_ref[...], preferred_element_type=F32, precision=lax.Precision.HIGHEST)
    tm = logits.shape[0]
    lane = lax.broadcasted_iota(jnp.int32, logits.shape, 1)
    neg = jnp.float32(-jnp.inf)

    def first_lane(mask):
        return jnp.min(jnp.where(mask, lane, LANES), axis=1, keepdims=True)

    is_g = lane < N_GROUPS
    lg = jnp.where(is_g, logits, neg)
    mg = jnp.max(lg, axis=1, keepdims=True)
    pg_sel = 1.0 / jnp.sum(jnp.exp(lg - mg), axis=1, keepdims=True)
    g_sel = first_lane(lg == mg)
    e_lo = N_GROUPS + EXPERTS_PER_GROUP * g_sel
    in_e = (lane >= e_lo) & (lane < e_lo + EXPERTS_PER_GROUP)
    le = jnp.where(in_e, logits, neg)
    me = jnp.max(le, axis=1, keepdims=True)
    ee = jnp.exp(le - me)
    pe = ee / jnp.sum(ee, axis=1, keepdims=True)
    pe = jnp.where(in_e, pe, -1.0)
    p0 = jnp.max(pe, axis=1, keepdims=True)
    i0 = first_lane(pe == p0)
    pe1 = jnp.where(lane == i0, -1.0, pe)
    p1 = jnp.max(pe1, axis=1, keepdims=True)
    i1 = first_lane(pe1 == p1)
    gate0 = pg_sel * p0 / (p0 + p1)
    gate1 = pg_sel * p1 / (p0 + p1)
    hit0, hit1 = lane == i0, lane == i1
    oh = jnp.concatenate([jnp.where(hit0, 1.0, 0.0), jnp.where(hit1, 1.0, 0.0)], axis=1)
    before = jnp.dot(tri_ref[...], oh.astype(MXU_DTYPE), preferred_element_type=F32)
    base = base_ref[...]
    tot0 = jnp.sum(oh[:, :LANES], axis=0, keepdims=True)
    tot1 = jnp.sum(oh[:, LANES:], axis=0, keepdims=True)
    rank0 = jnp.sum(jnp.where(hit0, before[:, :LANES] + base, 0.0), axis=1, keepdims=True)
    rank1 = jnp.sum(jnp.where(hit1, before[:, LANES:] + (base + tot0), 0.0), axis=1, keepdims=True)
    base = base + tot0 + tot1
    base_ref[...] = base
    cnt_ref[...] = base
    rec = jnp.zeros(logits.shape, F32)
    for pos, val in ((R_E0, (i0 - N_GROUPS).astype(F32)), (R_E1, (i1 - N_GROUPS).astype(F32)),
                     (R_RANK0, rank0), (R_RANK1, rank1), (R_GATE0, gate0), (R_GATE1, gate1)):
        rec = jnp.where(lane == pos, val, rec)
    rec_ref[...] = rec


def _router(x, g, w_router, tri):
    n, d = x.shape
    tm = TOKEN_TILE
    full = lambda a: pl.BlockSpec(a.shape, lambda i: (0,) * a.ndim)
    return pl.pallas_call(
        _router_kernel,
        grid=(n // tm,),
        in_specs=[pl.BlockSpec((tm, d), lambda i: (i, 0)), full(g), full(w_router), full(tri)],
        out_specs=[pl.BlockSpec((tm, LANES), lambda i: (i, 0)), pl.BlockSpec((1, LANES), lambda i: (0, 0))],
        out_shape=[jax.ShapeDtypeStruct((n, LANES), F32), jax.ShapeDtypeStruct((1, LANES), F32)],
        scratch_shapes=[pltpu.VMEM((1, LANES), F32)],
        compiler_params=_params(1), name="moe_router",
    )(x, g, w_router, tri)


ROW_TILE = 8


def _row_copy(src_ref, src_row, dst_ref, dst_row, sem):
    src = src_ref.at[pl.ds(pl.multiple_of(src_row * ROW_TILE, ROW_TILE), ROW_TILE), :]
    dst = dst_ref.at[pl.ds(pl.multiple_of(dst_row * ROW_TILE, ROW_TILE), ROW_TILE), :]
    return pltpu.make_async_copy(src, dst, sem)


def _to_row_tiles(ref, x):
    rows = x.shape[0]
    for c in range(ROW_TILE):
        ref[pl.ds(c, rows, stride=ROW_TILE), :] = x[:, c * LANES:(c + 1) * LANES]


def _from_row_tiles(ref):
    rows = ref.shape[0] // ROW_TILE
    return jnp.concatenate([ref[pl.ds(c, rows, stride=ROW_TILE), :] for c in range(ROW_TILE)], axis=1)


def _dispatch_kernel(dest_ref, x_ref, g_ref, xs_in_ref, xs_ref, xn_ref, sem, *, tm):
    del xs_in_ref
    _to_row_tiles(xn_ref, _rms(x_ref[...], g_ref[...], NORM_EPS))

    def issue(r, carry):
        for c in range(2):
            _row_copy(xn_ref, r, xs_ref, dest_ref[2 * r + c], sem).start()
        return carry
    lax.fori_loop(0, tm, issue, 0, unroll=8)

    def drain(r, carry):
        for c in range(2):
            _row_copy(xn_ref, 0, xs_ref, 0, sem).wait()
        return carry
    lax.fori_loop(0, tm, drain, 0, unroll=8)


def _dispatch(dest_flat, x, g, xs_zero):
    n, d = x.shape
    tm = TOKEN_TILE
    return pl.pallas_call(
        functools.partial(_dispatch_kernel, tm=tm),
        grid=(n // tm,),
        in_specs=[pl.BlockSpec((2 * tm,), lambda i: (i,), memory_space=pltpu.SMEM),
                  pl.BlockSpec((tm, d), lambda i: (i, 0)), pl.BlockSpec(g.shape, lambda i: (0, 0)),
                  pl.BlockSpec(memory_space=pl.ANY)],
        out_specs=pl.BlockSpec(memory_space=pl.ANY),
        out_shape=jax.ShapeDtypeStruct(xs_zero.shape, F32),
        scratch_shapes=[pltpu.VMEM((tm * ROW_TILE, LANES), F32), pltpu.SemaphoreType.DMA(())],
        input_output_aliases={3: 0},
        compiler_params=_params(1), name="moe_dispatch",
    )(dest_flat, x, g, xs_zero)


def _expert_kernel(be_ref, nv_ref, xs_ref, w1_ref, w3_ref, w2_ref, y_ref, w1c_ref, w3c_ref, w2c_ref):
    b = pl.program_id(0)

    @pl.when((b == 0) | (be_ref[b] != be_ref[jnp.maximum(b - 1, 0)]))
    def _():
        w1c_ref[...] = w1_ref[0, 0].astype(MXU_DTYPE)
        w3c_ref[...] = w3_ref[0, 0].astype(MXU_DTYPE)
        w2c_ref[...] = w2_ref[0, 0].astype(MXU_DTYPE)

    @pl.when(b < nv_ref[0])
    def _():
        xb = _from_row_tiles(xs_ref).astype(MXU_DTYPE)
        h1 = jnp.dot(xb, w1c_ref[...], preferred_element_type=F32)
        h3 = jnp.dot(xb, w3c_ref[...], preferred_element_type=F32)
        h = (h1 * jax.nn.sigmoid(h1) * h3).astype(MXU_DTYPE)
        _to_row_tiles(y_ref, jnp.dot(h, w2c_ref[...], preferred_element_type=F32))

    @pl.when(b >= nv_ref[0])
    def _():
        y_ref[...] = jnp.zeros(y_ref.shape, F32)


def _experts(block_expert, n_valid, xs, w1, w3, w2, layer):
    rb = EXPERT_ROWS
    d, de = w1.shape[2:]
    assert d == ROW_TILE * LANES
    return pl.pallas_call(
        _expert_kernel,
        grid_spec=pltpu.PrefetchScalarGridSpec(
            num_scalar_prefetch=2, grid=(xs.shape[0] // (rb * ROW_TILE),),
            in_specs=[pl.BlockSpec((rb * ROW_TILE, LANES), lambda b, be, nv: (b, 0)),
                      pl.BlockSpec((1, 1, d, de), lambda b, be, nv: (layer, be[b], 0, 0)),
                      pl.BlockSpec((1, 1, d, de), lambda b, be, nv: (layer, be[b], 0, 0)),
                      pl.BlockSpec((1, 1, de, d), lambda b, be, nv: (layer, be[b], 0, 0))],
            out_specs=pl.BlockSpec((rb * ROW_TILE, LANES), lambda b, be, nv: (b, 0)),
            scratch_shapes=[pltpu.VMEM((d, de), MXU_DTYPE), pltpu.VMEM((d, de), MXU_DTYPE),
                            pltpu.VMEM((de, d), MXU_DTYPE)]),
        out_shape=jax.ShapeDtypeStruct(xs.shape, F32),
        compiler_params=_params(1), name="moe_experts",
    )(block_expert, n_valid, xs, w1, w3, w2)


def _combine_kernel(dest_ref, x_ref, rec_ref, gfin_ref, yb_ref, o_ref, ybuf_ref, sem, *, tm, final):
    def issue(r, carry):
        for c in range(2):
            _row_copy(yb_ref, dest_ref[2 * r + c], ybuf_ref.at[c], r, sem).start()
        return carry
    lax.fori_loop(0, tm, issue, 0, unroll=8)

    def drain(r, carry):
        for c in range(2):
            _row_copy(yb_ref, 0, ybuf_ref.at[c], 0, sem).wait()
        return carry
    lax.fori_loop(0, tm, drain, 0, unroll=8)

    rec = rec_ref[...]
    out = (x_ref[...] + rec[:, R_GATE0:R_GATE0 + 1] * _from_row_tiles(ybuf_ref.at[0])
           + rec[:, R_GATE1:R_GATE1 + 1] * _from_row_tiles(ybuf_ref.at[1]))
    if final:
        out = _rms(out, gfin_ref[...], NORM_EPS)
    o_ref[...] = out


def _combine(dest_flat, x, rec, g_final, yb, final):
    n, d = x.shape
    tm = TOKEN_TILE
    return pl.pallas_call(
        functools.partial(_combine_kernel, tm=tm, final=final),
        grid=(n // tm,),
        in_specs=[pl.BlockSpec((2 * tm,), lambda i: (i,), memory_space=pltpu.SMEM),
                  pl.BlockSpec((tm, d), lambda i: (i, 0)), pl.BlockSpec((tm, LANES), lambda i: (i, 0)),
                  pl.BlockSpec(g_final.shape, lambda i: (0, 0)), pl.BlockSpec(memory_space=pl.ANY)],
        out_specs=pl.BlockSpec((tm, d), lambda i: (i, 0)),
        out_shape=jax.ShapeDtypeStruct((n, d), F32),
        scratch_shapes=[pltpu.VMEM((2, tm * ROW_TILE, LANES), F32), pltpu.SemaphoreType.DMA(())],
        compiler_params=_params(1), name="moe_combine",
    )(dest_flat, x, rec, g_final, yb)


def _moe(x, g, w_router, tri, w1, w3, w2, layer, g_final, final):
    n, d = x.shape
    rb = EXPERT_ROWS
    rec, counts = _router(x, g, w_router, tri)
    counts = counts[0, N_GROUPS:N_GROUPS + N_EXPERTS].astype(jnp.int32)
    padded = (counts + rb - 1) // rb * rb
    pad_end = jnp.cumsum(padded)
    pad_start = pad_end - padded
    n_blocks = (2 * n + N_EXPERTS * (rb - 1)) // rb
    expert = rec[:, R_E0:R_E1 + 1].astype(jnp.int32)
    rank = rec[:, R_RANK0:R_RANK1 + 1].astype(jnp.int32)
    before = expert[..., None] > jnp.arange(N_EXPERTS, dtype=jnp.int32)
    dest = (jnp.sum(jnp.where(before, padded, 0), axis=-1) + rank).reshape(-1)
    block_start = jnp.arange(n_blocks, dtype=jnp.int32) * rb
    block_expert = jnp.minimum(jnp.searchsorted(pad_end, block_start, side="right"), N_EXPERTS - 1).astype(jnp.int32)
    n_valid = (pad_end[-1:] // rb).astype(jnp.int32)
    xs = _dispatch(dest, x, g, jnp.zeros((n_blocks * rb * ROW_TILE, LANES), F32))
    yb = _experts(block_expert, n_valid, xs, w1, w3, w2, layer)
    return _combine(dest, x, rec, g_final, yb, final)


def _rope_tables(positions):
    inv = 1.0 / (ROPE_THETA ** (jnp.arange(0, MLA_ROPE, 2, dtype=F32) / MLA_ROPE))
    ang = positions.astype(F32)[:, None] * inv[None, :]
    cos = jnp.repeat(jnp.cos(ang), 2, axis=1)
    sin = jnp.stack([-jnp.sin(ang), jnp.sin(ang)], axis=-1).reshape(ang.shape[0], MLA_ROPE)
    pad = ((0, 0), (0, LANES - MLA_ROPE))
    return jnp.pad(cos, pad), jnp.pad(sin, pad)


def _pair_swap(w):
    return w.reshape(w.shape[0], -1, 2)[:, :, ::-1].reshape(w.shape)


def kernel(x_prompt, x_sample, ln_mix, ln_ffn, ln_final, da_wqkv, da_wo, da_lambda_q1, da_lambda_k1, da_lambda_q2, da_lambda_k2, da_subln, mla_w_down, mla_q_norm, mla_w_uq, mla_kv_norm, mla_w_ukv, mla_wo, moe_w_group, moe_w_expert, moe_w1, moe_w3, moe_w2):
    bp, sp, d = x_prompt.shape
    bs, ss, _ = x_sample.shape
    n_p, n_s = bp * sp, bs * ss
    groups = ((bp, sp, 0), (bs, ss, n_p))
    x = jnp.concatenate([x_prompt.reshape(n_p, d), x_sample.reshape(n_s, d)], axis=0)
    depth = ln_mix.shape[0]
    cdt = MXU_DTYPE

    positions = jnp.concatenate([jnp.tile(jnp.arange(sp), bp), jnp.tile(jnp.arange(ss), bs)])
    cos, sin = _rope_tables(positions)
    tri = jnp.tril(jnp.ones((TOKEN_TILE, TOKEN_TILE), F32), -1).astype(cdt)
    n_da_heads = d // (2 * DA_HEAD_DIM)
    slopes = 2.0 ** (-8.0 * np.arange(1, n_da_heads + 1) / n_da_heads) * LOG2E
    hd = MLA_HEADS * LANES
    zpad = LANES - MLA_ROPE
    rest, parts = jnp.asarray(slopes, F32), []
    for _ in range(SLOPE_PIECES):
        parts.append(rest.astype(jnp.bfloat16).astype(F32))
        rest = rest - parts[-1]
    slope_pieces = jnp.stack(parts, axis=1).reshape(-1)

    for i in range(depth):
        j = i // 2
        g_mix = ln_mix[i][None, :]
        if i % 2 == 0:
            lambda_init = 0.8 - 0.6 * math.exp(-0.3 * i)
            lam = (jnp.exp(jnp.sum(da_lambda_q1[j] * da_lambda_k1[j])) - jnp.exp(jnp.sum(da_lambda_q2[j] * da_lambda_k2[j]))
                   + lambda_init)
            scalars = jnp.concatenate([jnp.asarray(slopes, F32), lam[None].astype(F32), slope_pieces])
            w = da_wqkv[j]
            q, k, vt = _da_qkv(x, g_mix, w[:, :d].astype(cdt), w[:, d:2 * d].astype(cdt), w[:, 2 * d:].T.astype(cdt),
                               DA_HEAD_DIM ** -0.5 * LOG2E)
            oa, ob = [_da_attn(scalars, q, k, vt, da_subln[j][:, None], batch, seq, row0, 1.0 - lambda_init)
                      for batch, seq, row0 in groups]
            x = _out_proj(x, oa, ob, da_wo[j].astype(cdt))
        else:
            qr, kvr = MLA_Q_RANK, MLA_KV_RANK
            wd = mla_w_down[j]
            w_rope = wd[:, qr + kvr:]
            wd = jnp.concatenate([wd[:, :qr + kvr], jnp.pad(w_rope, ((0, 0), (0, zpad))),
                                  jnp.pad(_pair_swap(w_rope), ((0, 0), (0, zpad)))], axis=1)
            wuq = mla_w_uq[j].reshape(qr, MLA_HEADS, MLA_NOPE + MLA_ROPE)
            wq_rope = wuq[:, :, MLA_NOPE:]
            pad3 = ((0, 0), (0, 0), (0, zpad))
            wuq = jnp.concatenate([wuq[:, :, :MLA_NOPE].reshape(qr, hd), jnp.pad(wq_rope, pad3).reshape(qr, hd),
                                   jnp.pad(_pair_swap(wq_rope.reshape(qr, -1)).reshape(wq_rope.shape), pad3).reshape(qr, hd)],
                                  axis=1)
            wukv = mla_w_ukv[j].reshape(kvr, MLA_HEADS, MLA_NOPE + MLA_V)
            wkn = wukv[:, :, :MLA_NOPE].reshape(kvr, hd)
            wvt = wukv[:, :, MLA_NOPE:].reshape(kvr, MLA_HEADS * MLA_V).T
            q, kn, kr, vt = _mla_proj(x, g_mix, wd.astype(cdt), mla_q_norm[j][None, :], mla_kv_norm[j][None, :],
                                      wuq.astype(cdt), wkn.astype(cdt), wvt.astype(cdt), cos, sin,
                                      (MLA_NOPE + MLA_ROPE) ** -0.5 * LOG2E)
            oa, ob = [_mla_attn(q, kn, kr, vt, batch, seq, row0) for batch, seq, row0 in groups]
            x = _out_proj(x, oa, ob, mla_wo[j].astype(cdt))
        w_router = jnp.pad(jnp.concatenate([moe_w_group[i], moe_w_expert[i]], axis=1),
                           ((0, 0), (0, LANES - N_GROUPS - N_EXPERTS)))
        x = _moe(x, ln_ffn[i][None, :], w_router, tri, moe_w1, moe_w3, moe_w2, i, ln_final[None, :],
                 final=(i == depth - 1))
    return x[:n_p].reshape(bp, sp, d), x[n_p:].reshape(bs, ss, d)
```

```python
import functools
import math

import numpy as np
import jax
import jax.numpy as jnp
from jax import lax
from jax.experimental import pallas as pl
from jax.experimental.pallas import tpu as pltpu

F32 = jnp.float32
MXU_DTYPE = jnp.bfloat16
LOG2E = 1.4426950408889634

NORM_EPS = 1e-6
DA_SUBLN_EPS = 1e-5
ROPE_THETA = 10000.0
DA_HEAD_DIM = 64
MLA_HEADS = 8
MLA_NOPE = 128
MLA_ROPE = 64
MLA_V = 128
MLA_Q_RANK = 384
MLA_KV_RANK = 256
N_GROUPS = 8
EXPERTS_PER_GROUP = 8
N_EXPERTS = N_GROUPS * EXPERTS_PER_GROUP

LANES = 128
TOKEN_TILE = 512
KV_CHUNK = 512
DA_Q_TILE = 512
DA_COLS = 256
MLA_Q_TILE = 1024
MLA_COLS = 256
EXPERT_ROWS = 256
VMEM_LIMIT = 56 * 1024 * 1024


def _rms(x, g, eps):
    return x * lax.rsqrt(jnp.mean(x * x, axis=-1, keepdims=True) + eps) * g


def _nt_dot(a, b):
    return lax.dot_general(a, b, (((1,), (1,)), ((), ())), preferred_element_type=F32)


def _params(n_axes):
    return pltpu.CompilerParams(dimension_semantics=("arbitrary",) * n_axes, vmem_limit_bytes=VMEM_LIMIT)


def _da_qkv_kernel(x_ref, g_ref, wq_ref, wk_ref, wvt_ref, q_ref, k_ref, vt_ref, *, q_scale):
    xn = _rms(x_ref[...], g_ref[...], NORM_EPS).astype(MXU_DTYPE)
    q_ref[...] = (jnp.dot(xn, wq_ref[...], preferred_element_type=F32) * q_scale).astype(q_ref.dtype)
    k_ref[...] = jnp.dot(xn, wk_ref[...], preferred_element_type=F32).astype(k_ref.dtype)
    vt_ref[0] = _nt_dot(wvt_ref[...], xn).astype(vt_ref.dtype)


def _da_qkv(x, g, wq, wk, wvt, q_scale):
    n, d = x.shape
    tm = TOKEN_TILE
    full = lambda a: pl.BlockSpec(a.shape, lambda i: (0,) * a.ndim)
    return pl.pallas_call(
        functools.partial(_da_qkv_kernel, q_scale=q_scale),
        grid=(n // tm,),
        in_specs=[pl.BlockSpec((tm, d), lambda i: (i, 0)), full(g), full(wq), full(wk), full(wvt)],
        out_specs=[pl.BlockSpec((tm, d), lambda i: (i, 0)), pl.BlockSpec((tm, d), lambda i: (i, 0)),
                   pl.BlockSpec((1, d, tm), lambda i: (i, 0, 0))],
        out_shape=[jax.ShapeDtypeStruct((n, d), MXU_DTYPE), jax.ShapeDtypeStruct((n, d), MXU_DTYPE),
                   jax.ShapeDtypeStruct((n // tm, d, tm), MXU_DTYPE)],
        compiler_params=_params(1), name="da_qkv",
    )(x, g, wq, wk, wvt)


def _mla_proj_kernel(x_ref, g_ref, wd_ref, qg_ref, kvg_ref, wuq_ref, wkn_ref, wvt_ref, cos_ref, sin_ref,
                     q_ref, kn_ref, kr_ref, vt_ref, *, q_scale):
    xn = _rms(x_ref[...], g_ref[...], NORM_EPS).astype(MXU_DTYPE)
    down = jnp.dot(xn, wd_ref[...], preferred_element_type=F32)
    qr, kvr = MLA_Q_RANK, MLA_KV_RANK
    c_q = _rms(down[:, :qr], qg_ref[...], NORM_EPS).astype(MXU_DTYPE)
    c_kv = _rms(down[:, qr:qr + kvr], kvg_ref[...], NORM_EPS).astype(MXU_DTYPE)
    cos, sin = cos_ref[...], sin_ref[...]
    kr = down[:, qr + kvr:qr + kvr + LANES] * cos + down[:, qr + kvr + LANES:] * sin
    kr_ref[...] = kr.astype(kr_ref.dtype)
    hd = MLA_HEADS * LANES
    q = jnp.dot(c_q, wuq_ref[...], preferred_element_type=F32)
    q_ref[:, :hd] = (q[:, :hd] * q_scale).astype(q_ref.dtype)
    for h in range(MLA_HEADS):
        a = q[:, hd + h * LANES:hd + (h + 1) * LANES]
        b = q[:, 2 * hd + h * LANES:2 * hd + (h + 1) * LANES]
        q_ref[:, hd + h * LANES:hd + (h + 1) * LANES] = ((a * cos + b * sin) * q_scale).astype(q_ref.dtype)
    kn_ref[...] = jnp.dot(c_kv, wkn_ref[...], preferred_element_type=F32).astype(kn_ref.dtype)
    vt_ref[0] = _nt_dot(wvt_ref[...], c_kv).astype(vt_ref.dtype)


def _mla_proj(x, g, wd, qg, kvg, wuq, wkn, wvt, cos, sin, q_scale):
    n, d = x.shape
    tm = TOKEN_TILE
    hd = MLA_HEADS * LANES
    full = lambda a: pl.BlockSpec(a.shape, lambda i: (0,) * a.ndim)
    row = lambda w: pl.BlockSpec((tm, w), lambda i: (i, 0))
    return pl.pallas_call(
        functools.partial(_mla_proj_kernel, q_scale=q_scale),
        grid=(n // tm,),
        in_specs=[row(d), full(g), full(wd), full(qg), full(kvg), full(wuq), full(wkn), full(wvt),
                  row(LANES), row(LANES)],
        out_specs=[row(2 * hd), row(hd), row(LANES), pl.BlockSpec((1, hd, tm), lambda i: (i, 0, 0))],
        out_shape=[jax.ShapeDtypeStruct((n, 2 * hd), MXU_DTYPE), jax.ShapeDtypeStruct((n, hd), MXU_DTYPE),
                   jax.ShapeDtypeStruct((n, LANES), MXU_DTYPE),
                   jax.ShapeDtypeStruct((n // tm, hd, tm), MXU_DTYPE)],
        compiler_params=_params(1), name="mla_proj",
    )(x, g, wd, qg, kvg, wuq, wkn, wvt, cos, sin)


def _out_proj_kernel(x_ref, oa_ref, ob_ref, w_ref, y_ref, *, a_tiles):
    @pl.when(pl.program_id(0) < a_tiles)
    def _():
        y_ref[...] = x_ref[...] + jnp.dot(oa_ref[...], w_ref[...], preferred_element_type=F32)

    @pl.when(pl.program_id(0) >= a_tiles)
    def _():
        y_ref[...] = x_ref[...] + jnp.dot(ob_ref[...], w_ref[...], preferred_element_type=F32)


def _out_proj(x, oa, ob, w):
    n, d = x.shape
    tm = TOKEN_TILE
    a_tiles = oa.shape[0] // tm
    assert oa.shape[0] % tm == 0 and ob.shape[0] % tm == 0 and oa.shape[0] + ob.shape[0] == n
    return pl.pallas_call(
        functools.partial(_out_proj_kernel, a_tiles=a_tiles),
        grid=(n // tm,),
        in_specs=[pl.BlockSpec((tm, d), lambda i: (i, 0)),
                  pl.BlockSpec((tm, oa.shape[1]), lambda i: (jnp.minimum(i, a_tiles - 1), 0)),
                  pl.BlockSpec((tm, ob.shape[1]), lambda i: (jnp.maximum(i - a_tiles, 0), 0)),
                  pl.BlockSpec(w.shape, lambda i: (0, 0))],
        out_specs=pl.BlockSpec((tm, d), lambda i: (i, 0)),
        out_shape=jax.ShapeDtypeStruct((n, d), F32),
        compiler_params=_params(1), name="out_proj",
    )(x, oa, ob, w)


FLASH_STRIP = 32
FRAME_HEADROOM = 60.0
FLASH_ROUNDS = 2
FLASH_SLOTS = 3


def _flash_scratch(tk, w, dv):
    n = FLASH_SLOTS
    return ([pltpu.VMEM((tk, w), F32)] * n + [pltpu.VMEM((tk, w), MXU_DTYPE)] * n + [pltpu.VMEM((1, w), F32)] * (2 * n)
            + [pltpu.VMEM((1, w), F32), pltpu.VMEM((1, w), F32), pltpu.VMEM((dv, w), F32)])


class _Frame:
    def __init__(self, order, operands, weighted):
        self.order, self.operands, self.weighted = order, operands, weighted


def _flash_t(q, k_ref, vt_ref, n_chunks, tk, scratch, bias_fn, cols, frame):
    n = FLASH_SLOTS
    s_bufs, p_bufs, c_bufs, a_bufs = (scratch[i * n:(i + 1) * n] for i in range(4))
    m_ref, l_ref, acc_ref = scratch[4 * n:4 * n + 3]
    assert n_chunks >= 2
    w = q.shape[0]
    groups = [slice(g, g + cols) for g in range(0, w, cols)]

    def scores_g(c, slot, g):
        k = k_ref[pl.ds(pl.multiple_of(c * tk, tk), tk), :]
        st = _nt_dot(k, q[g, :])
        if bias_fn is not None:
            st = bias_fn(st, c, g)
        cmax = None
        for r in range(0, tk, FLASH_STRIP):
            strip = st[r:r + FLASH_STRIP, :]
            s_bufs[slot][r:r + FLASH_STRIP, g] = strip
            cmax = strip if cmax is None else jnp.maximum(cmax, strip)
        c_bufs[slot][:, g] = jnp.max(cmax, axis=0, keepdims=True)

    def softmax_g(slot, g):
        m_prev = m_ref[:, g]
        m_new = jnp.maximum(m_prev, c_bufs[slot][:, g])
        alpha = jnp.exp2(m_prev - m_new)
        m_ref[:, g] = m_new
        a_bufs[slot][:, g] = alpha
        psum = None
        for r in range(0, tk, FLASH_STRIP):
            p = jnp.exp2(s_bufs[slot][r:r + FLASH_STRIP, g] - m_new)
            psum = p if psum is None else psum + p
            p_bufs[slot][r:r + FLASH_STRIP, g] = p.astype(MXU_DTYPE)
        l_ref[:, g] = alpha * l_ref[:, g] + jnp.sum(psum, axis=0, keepdims=True)

    def values_g(c, slot, g):
        tv = vt_ref.shape[2]
        pv = None
        for j in range(tk // tv):
            part = jnp.dot(vt_ref[c * (tk // tv) + j], p_bufs[slot][j * tv:(j + 1) * tv, g],
                           preferred_element_type=F32)
            pv = part if pv is None else pv + part
        acc_ref[:, g] = acc_ref[:, g] * a_bufs[slot][:, g] + pv

    def scores(c, slot):
        for g in groups:
            scores_g(c, slot, g)

    def softmax(slot):
        for g in groups:
            softmax_g(slot, g)

    def values(c, slot):
        for g in groups:
            values_g(c, slot, g)

    def step(c, slot):
        for g in groups:
            values_g(c - 1, (slot - 1) % n, g)
            scores_g(c + 1, (slot + 1) % n, g)
            softmax_g(slot, g)

    per_trip = n * FLASH_ROUNDS

    def rescaling_pass():
        m_ref[...] = jnp.full(m_ref.shape, -jnp.inf, F32)
        l_ref[...] = jnp.zeros(l_ref.shape, F32)
        acc_ref[...] = jnp.zeros(acc_ref.shape, F32)
        scores(0, 0)
        scores(1, 1)
        softmax(0)
        trips = (n_chunks - 2) // per_trip

        def body(j, carry):
            for r in range(per_trip):
                step(1 + per_trip * j + r, (1 + r) % n)
            return carry

        lax.fori_loop(0, trips, body, 0)
        for c in range(1 + per_trip * trips, n_chunks - 1):
            step(c, c % n)
        values(n_chunks - 2, (n_chunks - 2) % n)
        softmax((n_chunks - 1) % n)
        values(n_chunks - 1, (n_chunks - 1) % n)

    if frame is None:
        rescaling_pass()
        return

    excess_ref = c_bufs[0]

    def frame_values_g(c, slot, g):
        tv = vt_ref.shape[2]
        pv = None
        for j in range(tk // tv):
            part = jnp.dot(vt_ref[c * (tk // tv) + j], p_bufs[slot][j * tv:(j + 1) * tv, g],
                           preferred_element_type=F32)
            pv = part if pv is None else pv + part
        if frame.weighted:
            pv = pv * a_bufs[slot][:, g]
        acc_ref[:, g] = acc_ref[:, g] + pv

    def frame_scores_g(c, slot, g):
        qg, k_extra, shift, weight = frame.operands(c, g)
        k = k_ref[pl.ds(pl.multiple_of(c * tk, tk), tk), :]
        if k_extra is not None:
            k = jnp.concatenate([k, k_extra], axis=1)
        st = _nt_dot(k, qg)
        base = m_ref[:, g] if shift is None else m_ref[:, g] + shift
        cmax = psum = None
        for r in range(0, tk, FLASH_STRIP):
            strip = st[r:r + FLASH_STRIP, :]
            cmax = strip if cmax is None else jnp.maximum(cmax, strip)
            p = jnp.exp2(strip - base)
            psum = p if psum is None else psum + p
            p_bufs[slot][r:r + FLASH_STRIP, g] = p.astype(MXU_DTYPE)
        excess_ref[:, g] = jnp.maximum(excess_ref[:, g], jnp.max(cmax, axis=0, keepdims=True) - base)
        csum = jnp.sum(psum, axis=0, keepdims=True)
        if frame.weighted:
            a_bufs[slot][:, g] = weight
            csum = csum * weight
        l_ref[:, g] = l_ref[:, g] + csum

    def frame_step(j, slot):
        for g in groups:
            frame_values_g(frame.order(j - 1), 1 - slot, g)
            frame_scores_g(frame.order(j), slot, g)

    acc_ref[...] = jnp.zeros(acc_ref.shape, F32)
    excess_ref[...] = jnp.zeros(excess_ref.shape, F32)
    c0 = frame.order(0)
    for g in groups:
        k = k_ref[pl.ds(pl.multiple_of(c0 * tk, tk), tk), :]
        st = _nt_dot(k, q[g, :])
        if bias_fn is not None:
            st = bias_fn(st, c0, g)
        top = jnp.max(st, axis=0, keepdims=True)
        m_ref[:, g] = top
        p = jnp.exp2(st - top)
        l_ref[:, g] = jnp.sum(p, axis=0, keepdims=True)
        p_bufs[0][:, g] = p.astype(MXU_DTYPE)
        if frame.weighted:
            a_bufs[0][:, g] = jnp.ones((1, cols), F32)
    frame_trips = (n_chunks - 1) // per_trip

    def frame_body(t, carry):
        for r in range(per_trip):
            frame_step(1 + per_trip * t + r, (1 + r) % 2)
        return carry

    assert per_trip % 2 == 0
    lax.fori_loop(0, frame_trips, frame_body, 0)
    for j in range(1 + per_trip * frame_trips, n_chunks):
        frame_step(j, j % 2)
    for g in groups:
        frame_values_g(frame.order(n_chunks - 1), (n_chunks - 1) % 2, g)

    @pl.when(jnp.max(excess_ref[...]) > FRAME_HEADROOM)
    def _():
        rescaling_pass()


POS_SPLIT = 128
SLOPE_PIECES = 3


def _row_index_columns(tk, dtype):
    i = np.arange(tk)
    cols = np.zeros((tk, LANES), np.float32)
    for piece in range(SLOPE_PIECES):
        cols[:, 2 * piece] = i // POS_SPLIT
        cols[:, 2 * piece + 1] = i % POS_SPLIT
    return jnp.asarray(cols, dtype)


def _kv_chunk(seq):
    return min(KV_CHUNK, max(TOKEN_TILE, seq // 2))


def _da_attn_kernel(sc_ref, q_ref, k_ref, vt_ref, g_ref, pos_ref, o_ref, dist_ref, qa_ref, *scratch,
                    tq, tk, n_chunks, n_heads, out_scale):
    l_ref, acc_ref = scratch[4 * FLASH_SLOTS + 1:]
    h, qi = pl.program_id(1), pl.program_id(2)
    slope, lam = sc_ref[h], sc_ref[n_heads]
    q = q_ref[...]
    lane = lax.broadcasted_iota(jnp.int32, q.shape, 1)
    zero = jnp.zeros_like(q)
    d = q.shape[1] // 2
    qs = jnp.concatenate([jnp.where(lane < d, q, zero), jnp.where(lane >= d, q, zero)], axis=0)
    row = lax.broadcasted_iota(jnp.int32, (tk, tq), 0)
    col = lax.broadcasted_iota(jnp.int32, (tk, tq), 1)
    dist_ref[...] = (row - col - qi * tq).astype(F32)

    assert tq % DA_COLS == 0

    def bias_fn(st, c, g):
        off = g.start % tq
        dist = dist_ref[:, off:off + DA_COLS] + lax.convert_element_type(c * tk, F32)
        return st - slope * jnp.abs(dist)

    pieces = [sc_ref[n_heads + 1 + SLOPE_PIECES * h + i] for i in range(SLOPE_PIECES)]
    coef = jnp.zeros(qs.shape, F32)
    lane2 = lax.broadcasted_iota(jnp.int32, qs.shape, 1)
    for i, piece in enumerate(pieces):
        coef = jnp.where(lane2 == 2 * i, piece * float(POS_SPLIT), jnp.where(lane2 == 2 * i + 1, piece, coef))
    qa_ref[0] = jnp.concatenate([qs, (-coef).astype(qs.dtype)], axis=1)
    qa_ref[1] = jnp.concatenate([qs, coef.astype(qs.dtype)], axis=1)
    home = (qi * tq) // tk
    qpos = (qi * tq + lax.broadcasted_iota(jnp.int32, (1, tq), 1)).astype(F32)

    def operands(c, g):
        before = c < home
        start = lax.convert_element_type(c * tk, F32)
        shift = jnp.where(before, slope * (tk - 1), 0.0)
        edge = jnp.where(before, (start + (tk - 1)) - qpos, qpos - start)
        weight = jnp.exp2(slope * edge)
        weight = weight[:, g.start % tq:g.start % tq + DA_COLS]
        return qa_ref[before.astype(jnp.int32)][g, :], pos_ref[...], shift, weight

    frame = _Frame(order=lambda j: lax.rem(home + j, jnp.int32(n_chunks)), operands=operands, weighted=True)
    _flash_t(qs, k_ref, vt_ref, n_chunks, tk, scratch, bias_fn, DA_COLS, frame)
    o = acc_ref[...] * (1.0 / l_ref[...])
    o = o[:, :tq] - lam * o[:, tq:]
    o = o * lax.rsqrt(jnp.mean(o * o, axis=0, keepdims=True) + DA_SUBLN_EPS) * g_ref[...] * out_scale
    o_ref[...] = o.T.astype(o_ref.dtype)


def _da_attn(scalars, q, k, vt, subln_col, batch, seq, row0, out_scale):
    dm = q.shape[1]
    tq, tk, tv = DA_Q_TILE, _kv_chunk(seq), vt.shape[2]
    hw = 2 * DA_HEAD_DIM
    n_heads = dm // hw
    assert row0 % seq == 0 and seq % tq == 0 and seq % tk == 0 and tk % tv == 0
    s0, q0, nq, n_chunks = row0 // seq, row0 // tq, seq // tq, seq // tk
    assert tk % tq == 0 and tk % POS_SPLIT == 0 and tk // POS_SPLIT <= 256
    pos = _row_index_columns(tk, q.dtype)
    return pl.pallas_call(
        functools.partial(_da_attn_kernel, tq=tq, tk=tk, n_chunks=n_chunks, n_heads=n_heads, out_scale=out_scale),
        grid_spec=pltpu.PrefetchScalarGridSpec(
            num_scalar_prefetch=1, grid=(batch, n_heads, nq),
            in_specs=[pl.BlockSpec((tq, hw), lambda b, h, i, sc: (q0 + b * nq + i, h)),
                      pl.BlockSpec((seq, hw), lambda b, h, i, sc: (s0 + b, h)),
                      pl.BlockSpec((seq // tv, hw, tv), lambda b, h, i, sc: (s0 + b, h, 0)),
                      pl.BlockSpec(subln_col.shape, lambda b, h, i, sc: (0, 0)),
                      pl.BlockSpec(pos.shape, lambda b, h, i, sc: (0, 0))],
            out_specs=pl.BlockSpec((tq, hw), lambda b, h, i, sc: (b * nq + i, h)),
            scratch_shapes=[pltpu.VMEM((tk, tq), F32), pltpu.VMEM((2, 2 * tq, hw + LANES), q.dtype)]
            + _flash_scratch(tk, 2 * tq, hw)),
        out_shape=jax.ShapeDtypeStruct((batch * seq, dm), MXU_DTYPE),
        compiler_params=_params(3), name="da_attn",
    )(scalars, q, k, vt, subln_col, pos)


def _mla_attn_kernel(qn_ref, qr_ref, kn_ref, kr_ref, vt_ref, o_ref, kfull_ref, *scratch, tk, n_chunks):
    l_ref, acc_ref = scratch[4 * FLASH_SLOTS + 1:]

    @pl.when(pl.program_id(2) == 0)
    def _():
        def copy(c, carry):
            rows = pl.ds(pl.multiple_of(c * tk, tk), tk)
            kfull_ref[rows, :LANES] = kn_ref[rows, :]
            kfull_ref[rows, LANES:] = kr_ref[rows, :]
            return carry
        lax.fori_loop(0, n_chunks, copy, 0)

    q = jnp.concatenate([qn_ref[...], qr_ref[...]], axis=1)
    frame = _Frame(order=lambda j: j, operands=lambda c, g: (q[g, :], None, None, None), weighted=False)
    _flash_t(q, kfull_ref, vt_ref, n_chunks, tk, scratch, None, MLA_COLS, frame)
    o = acc_ref[...] * (1.0 / l_ref[...])
    o_ref[...] = o.T.astype(o_ref.dtype)


def _mla_attn(q, kn, kr, vt, batch, seq, row0):
    n_heads = MLA_HEADS
    tq, tk, tv = MLA_Q_TILE, _kv_chunk(seq), vt.shape[2]
    assert row0 % seq == 0 and seq % tq == 0 and seq % tk == 0 and tk % tv == 0
    s0, q0, nq, n_chunks = row0 // seq, row0 // tq, seq // tq, seq // tk
    return pl.pallas_call(
        functools.partial(_mla_attn_kernel, tk=tk, n_chunks=n_chunks),
        grid=(batch, n_heads, nq),
        in_specs=[pl.BlockSpec((tq, LANES), lambda b, h, i: (q0 + b * nq + i, h)),
                  pl.BlockSpec((tq, LANES), lambda b, h, i: (q0 + b * nq + i, n_heads + h)),
                  pl.BlockSpec((seq, LANES), lambda b, h, i: (s0 + b, h)),
                  pl.BlockSpec((seq, LANES), lambda b, h, i: (s0 + b, 0)),
                  pl.BlockSpec((seq // tv, MLA_V, tv), lambda b, h, i: (s0 + b, h, 0))],
        out_specs=pl.BlockSpec((tq, MLA_V), lambda b, h, i: (b * nq + i, h)),
        out_shape=jax.ShapeDtypeStruct((batch * seq, n_heads * MLA_V), MXU_DTYPE),
        scratch_shapes=[pltpu.VMEM((seq, 2 * LANES), MXU_DTYPE)] + _flash_scratch(tk, tq, MLA_V),
        compiler_params=_params(3), name="mla_attn",
    )(q, q, kn, kr, vt)


R_E0, R_E1, R_RANK0, R_RANK1, R_GATE0, R_GATE1 = range(6)


def _router_kernel(x_ref, g_ref, w_ref, tri_ref, rec_ref, cnt_ref, base_ref):
    i = pl.program_id(0)

    @pl.when(i == 0)
    def _():
        base_ref[...] = jnp.zeros(base_ref.shape, F32)

    xn = _rms(x_ref[...], g_ref[...], NORM_EPS)
    logits = jnp.dot(xn, w_ref[...], preferred_element_type=F32, precision=lax.Precision.HIGHEST)
    tm = logits.shape[0]
    lane = lax.broadcasted_iota(jnp.int32, logits.shape, 1)
    neg = jnp.float32(-jnp.inf)

    def first_lane(mask):
        return jnp.min(jnp.where(mask, lane, LANES), axis=1, keepdims=True)

    is_g = lane < N_GROUPS
    lg = jnp.where(is_g, logits, neg)
    mg = jnp.max(lg, axis=1, keepdims=True)
    pg_sel = 1.0 / jnp.sum(jnp.exp(lg - mg), axis=1, keepdims=True)
    g_sel = first_lane(lg == mg)
    e_lo = N_GROUPS + EXPERTS_PER_GROUP * g_sel
    in_e = (lane >= e_lo) & (lane < e_lo + EXPERTS_PER_GROUP)
    le = jnp.where(in_e, logits, neg)
    me = jnp.max(le, axis=1, keepdims=True)
    ee = jnp.exp(le - me)
    pe = ee / jnp.sum(ee, axis=1, keepdims=True)
    pe = jnp.where(in_e, pe, -1.0)
    p0 = jnp.max(pe, axis=1, keepdims=True)
    i0 = first_lane(pe == p0)
    pe1 = jnp.where(lane == i0, -1.0, pe)
    p1 = jnp.max(pe1, axis=1, keepdims=True)
    i1 = first_lane(pe1 == p1)
    gate0 = pg_sel * p0 / (p0 + p1)
    gate1 = pg_sel * p1 / (p0 + p1)
    hit0, hit1 = lane == i0, lane == i1
    oh = jnp.concatenate([jnp.where(hit0, 1.0, 0.0), jnp.where(hit1, 1.0, 0.0)], axis=1)
    before = jnp.dot(tri_ref[...], oh.astype(MXU_DTYPE), preferred_element_type=F32)
    base = base_ref[...]
    tot0 = jnp.sum(oh[:, :LANES], axis=0, keepdims=True)
    tot1 = jnp.sum(oh[:, LANES:], axis=0, keepdims=True)
    rank0 = jnp.sum(jnp.where(hit0, before[:, :LANES] + base, 0.0), axis=1, keepdims=True)
    rank1 = jnp.sum(jnp.where(hit1, before[:, LANES:] + (base + tot0), 0.0), axis=1, keepdims=True)
    base = base + tot0 + tot1
    base_ref[...] = base
    cnt_ref[...] = base
    rec = jnp.zeros(logits.shape, F32)
    for pos, val in ((R_E0, (i0 - N_GROUPS).astype(F32)), (R_E1, (i1 - N_GROUPS).astype(F32)),
                     (R_RANK0, rank0), (R_RANK1, rank1), (R_GATE0, gate0), (R_GATE1, gate1)):
        rec = jnp.where(lane == pos, val, rec)
    rec_ref[...] = rec


def _router(x, g, w_router, tri):
    n, d = x.shape
    tm = TOKEN_TILE
    full = lambda a: pl.BlockSpec(a.shape, lambda i: (0,) * a.ndim)
    return pl.pallas_call(
        _router_kernel,
        grid=(n // tm,),
        in_specs=[pl.BlockSpec((tm, d), lambda i: (i, 0)), full(g), full(w_router), full(tri)],
        out_specs=[pl.BlockSpec((tm, LANES), lambda i: (i, 0)), pl.BlockSpec((1, LANES), lambda i: (0, 0))],
        out_shape=[jax.ShapeDtypeStruct((n, LANES), F32), jax.ShapeDtypeStruct((1, LANES), F32)],
        scratch_shapes=[pltpu.VMEM((1, LANES), F32)],
        compiler_params=_params(1), name="moe_router",
    )(x, g, w_router, tri)


ROW_TILE = 8


def _row_copy(src_ref, src_row, dst_ref, dst_row, sem):
    src = src_ref.at[pl.ds(pl.multiple_of(src_row * ROW_TILE, ROW_TILE), ROW_TILE), :]
    dst = dst_ref.at[pl.ds(pl.multiple_of(dst_row * ROW_TILE, ROW_TILE), ROW_TILE), :]
    return pltpu.make_async_copy(src, dst, sem)


def _to_row_tiles(ref, x):
    rows = x.shape[0]
    for c in range(ROW_TILE):
        ref[pl.ds(c, rows, stride=ROW_TILE), :] = x[:, c * LANES:(c + 1) * LANES]


def _from_row_tiles(ref):
    rows = ref.shape[0] // ROW_TILE
    return jnp.concatenate([ref[pl.ds(c, rows, stride=ROW_TILE), :] for c in range(ROW_TILE)], axis=1)


def _dispatch_kernel(dest_ref, x_ref, g_ref, xs_in_ref, xs_ref, xn_ref, sem, *, tm):
    del xs_in_ref
    _to_row_tiles(xn_ref, _rms(x_ref[...], g_ref[...], NORM_EPS))

    def issue(r, carry):
        for c in range(2):
            _row_copy(xn_ref, r, xs_ref, dest_ref[2 * r + c], sem).start()
        return carry
    lax.fori_loop(0, tm, issue, 0, unroll=8)

    def drain(r, carry):
        for c in range(2):
            _row_copy(xn_ref, 0, xs_ref, 0, sem).wait()
        return carry
    lax.fori_loop(0, tm, drain, 0, unroll=8)


def _dispatch(dest_flat, x, g, xs_zero):
    n, d = x.shape
    tm = TOKEN_TILE
    return pl.pallas_call(
        functools.partial(_dispatch_kernel, tm=tm),
        grid=(n // tm,),
        in_specs=[pl.BlockSpec((2 * tm,), lambda i: (i,), memory_space=pltpu.SMEM),
                  pl.BlockSpec((tm, d), lambda i: (i, 0)), pl.BlockSpec(g.shape, lambda i: (0, 0)),
                  pl.BlockSpec(memory_space=pl.ANY)],
        out_specs=pl.BlockSpec(memory_space=pl.ANY),
        out_shape=jax.ShapeDtypeStruct(xs_zero.shape, F32),
        scratch_shapes=[pltpu.VMEM((tm * ROW_TILE, LANES), F32), pltpu.SemaphoreType.DMA(())],
        input_output_aliases={3: 0},
        compiler_params=_params(1), name="moe_dispatch",
    )(dest_flat, x, g, xs_zero)


def _expert_kernel(be_ref, nv_ref, xs_ref, w1_ref, w3_ref, w2_ref, y_ref, w1c_ref, w3c_ref, w2c_ref):
    b = pl.program_id(0)

    @pl.when((b == 0) | (be_ref[b] != be_ref[jnp.maximum(b - 1, 0)]))
    def _():
        w1c_ref[...] = w1_ref[0, 0].astype(MXU_DTYPE)
        w3c_ref[...] = w3_ref[0, 0].astype(MXU_DTYPE)
        w2c_ref[...] = w2_ref[0, 0].astype(MXU_DTYPE)

    @pl.when(b < nv_ref[0])
    def _():
        xb = _from_row_tiles(xs_ref).astype(MXU_DTYPE)
        h1 = jnp.dot(xb, w1c_ref[...], preferred_element_type=F32)
        h3 = jnp.dot(xb, w3c_ref[...], preferred_element_type=F32)
        h = (h1 * jax.nn.sigmoid(h1) * h3).astype(MXU_DTYPE)
        _to_row_tiles(y_ref, jnp.dot(h, w2c_ref[...], preferred_element_type=F32))

    @pl.when(b >= nv_ref[0])
    def _():
        y_ref[...] = jnp.zeros(y_ref.shape, F32)


def _experts(block_expert, n_valid, xs, w1, w3, w2, layer):
    rb = EXPERT_ROWS
    d, de = w1.shape[2:]
    assert d == ROW_TILE * LANES
    return pl.pallas_call(
        _expert_kernel,
        grid_spec=pltpu.PrefetchScalarGridSpec(
            num_scalar_prefetch=2, grid=(xs.shape[0] // (rb * ROW_TILE),),
            in_specs=[pl.BlockSpec((rb * ROW_TILE, LANES), lambda b, be, nv: (b, 0)),
                      pl.BlockSpec((1, 1, d, de), lambda b, be, nv: (layer, be[b], 0, 0)),
                      pl.BlockSpec((1, 1, d, de), lambda b, be, nv: (layer, be[b], 0, 0)),
                      pl.BlockSpec((1, 1, de, d), lambda b, be, nv: (layer, be[b], 0, 0))],
            out_specs=pl.BlockSpec((rb * ROW_TILE, LANES), lambda b, be, nv: (b, 0)),
            scratch_shapes=[pltpu.VMEM((d, de), MXU_DTYPE), pltpu.VMEM((d, de), MXU_DTYPE),
                            pltpu.VMEM((de, d), MXU_DTYPE)]),
        out_shape=jax.ShapeDtypeStruct(xs.shape, F32),
        compiler_params=_params(1), name="moe_experts",
    )(block_expert, n_valid, xs, w1, w3, w2)


def _combine_kernel(dest_ref, x_ref, rec_ref, gfin_ref, yb_ref, o_ref, ybuf_ref, sem, *, tm, final):
    def issue(r, carry):
        for c in range(2):
            _row_copy(yb_ref, dest_ref[2 * r + c], ybuf_ref.at[c], r, sem).start()
        return carry
    lax.fori_loop(0, tm, issue, 0, unroll=8)

    def drain(r, carry):
        for c in range(2):
            _row_copy(yb_ref, 0, ybuf_ref.at[c], 0, sem).wait()
        return carry
    lax.fori_loop(0, tm, drain, 0, unroll=8)

    rec = rec_ref[...]
    out = (x_ref[...] + rec[:, R_GATE0:R_GATE0 + 1] * _from_row_tiles(ybuf_ref.at[0])
           + rec[:, R_GATE1:R_GATE1 + 1] * _from_row_tiles(ybuf_ref.at[1]))
    if final:
        out = _rms(out, gfin_ref[...], NORM_EPS)
    o_ref[...] = out


def _combine(dest_flat, x, rec, g_final, yb, final):
    n, d = x.shape
    tm = TOKEN_TILE
    return pl.pallas_call(
        functools.partial(_combine_kernel, tm=tm, final=final),
        grid=(n // tm,),
        in_specs=[pl.BlockSpec((2 * tm,), lambda i: (i,), memory_space=pltpu.SMEM),
                  pl.BlockSpec((tm, d), lambda i: (i, 0)), pl.BlockSpec((tm, LANES), lambda i: (i, 0)),
                  pl.BlockSpec(g_final.shape, lambda i: (0, 0)), pl.BlockSpec(memory_space=pl.ANY)],
        out_specs=pl.BlockSpec((tm, d), lambda i: (i, 0)),
        out_shape=jax.ShapeDtypeStruct((n, d), F32),
        scratch_shapes=[pltpu.VMEM((2, tm * ROW_TILE, LANES), F32), pltpu.SemaphoreType.DMA(())],
        compiler_params=_params(1), name="moe_combine",
    )(dest_flat, x, rec, g_final, yb)


def _moe(x, g, w_router, tri, w1, w3, w2, layer, g_final, final):
    n, d = x.shape
    rb = EXPERT_ROWS
    rec, counts = _router(x, g, w_router, tri)
    counts = counts[0, N_GROUPS:N_GROUPS + N_EXPERTS].astype(jnp.int32)
    padded = (counts + rb - 1) // rb * rb
    pad_end = jnp.cumsum(padded)
    pad_start = pad_end - padded
    n_blocks = (2 * n + N_EXPERTS * (rb - 1)) // rb
    expert = rec[:, R_E0:R_E1 + 1].astype(jnp.int32)
    rank = rec[:, R_RANK0:R_RANK1 + 1].astype(jnp.int32)
    before = expert[..., None] > jnp.arange(N_EXPERTS, dtype=jnp.int32)
    dest = (jnp.sum(jnp.where(before, padded, 0), axis=-1) + rank).reshape(-1)
    block_start = jnp.arange(n_blocks, dtype=jnp.int32) * rb
    block_expert = jnp.minimum(jnp.searchsorted(pad_end, block_start, side="right"), N_EXPERTS - 1).astype(jnp.int32)
    n_valid = (pad_end[-1:] // rb).astype(jnp.int32)
    xs = _dispatch(dest, x, g, jnp.zeros((n_blocks * rb * ROW_TILE, LANES), F32))
    yb = _experts(block_expert, n_valid, xs, w1, w3, w2, layer)
    return _combine(dest, x, rec, g_final, yb, final)


def _rope_tables(positions):
    inv = 1.0 / (ROPE_THETA ** (jnp.arange(0, MLA_ROPE, 2, dtype=F32) / MLA_ROPE))
    ang = positions.astype(F32)[:, None] * inv[None, :]
    cos = jnp.repeat(jnp.cos(ang), 2, axis=1)
    sin = jnp.stack([-jnp.sin(ang), jnp.sin(ang)], axis=-1).reshape(ang.shape[0], MLA_ROPE)
    pad = ((0, 0), (0, LANES - MLA_ROPE))
    return jnp.pad(cos, pad), jnp.pad(sin, pad)


def _pair_swap(w):
    return w.reshape(w.shape[0], -1, 2)[:, :, ::-1].reshape(w.shape)


def kernel(x_prompt, x_sample, ln_mix, ln_ffn, ln_final, da_wqkv, da_wo, da_lambda_q1, da_lambda_k1, da_lambda_q2, da_lambda_k2, da_subln, mla_w_down, mla_q_norm, mla_w_uq, mla_kv_norm, mla_w_ukv, mla_wo, moe_w_group, moe_w_expert, moe_w1, moe_w3, moe_w2):
    bp, sp, d = x_prompt.shape
    bs, ss, _ = x_sample.shape
    n_p, n_s = bp * sp, bs * ss
    groups = ((bp, sp, 0), (bs, ss, n_p))
    x = jnp.concatenate([x_prompt.reshape(n_p, d), x_sample.reshape(n_s, d)], axis=0)
    depth = ln_mix.shape[0]
    cdt = MXU_DTYPE

    positions = jnp.concatenate([jnp.tile(jnp.arange(sp), bp), jnp.tile(jnp.arange(ss), bs)])
    cos, sin = _rope_tables(positions)
    tri = jnp.tril(jnp.ones((TOKEN_TILE, TOKEN_TILE), F32), -1).astype(cdt)
    n_da_heads = d // (2 * DA_HEAD_DIM)
    slopes = 2.0 ** (-8.0 * np.arange(1, n_da_heads + 1) / n_da_heads) * LOG2E
    hd = MLA_HEADS * LANES
    zpad = LANES - MLA_ROPE
    rest, parts = jnp.asarray(slopes, F32), []
    for _ in range(SLOPE_PIECES):
        parts.append(rest.astype(jnp.bfloat16).astype(F32))
        rest = rest - parts[-1]
    slope_pieces = jnp.stack(parts, axis=1).reshape(-1)

    for i in range(depth):
        j = i // 2
        g_mix = ln_mix[i][None, :]
        if i % 2 == 0:
            lambda_init = 0.8 - 0.6 * math.exp(-0.3 * i)
            lam = (jnp.exp(jnp.sum(da_lambda_q1[j] * da_lambda_k1[j])) - jnp.exp(jnp.sum(da_lambda_q2[j] * da_lambda_k2[j]))
                   + lambda_init)
            scalars = jnp.concatenate([jnp.asarray(slopes, F32), lam[None].astype(F32), slope_pieces])
            w = da_wqkv[j]
            q, k, vt = _da_qkv(x, g_mix, w[:, :d].astype(cdt), w[:, d:2 * d].astype(cdt), w[:, 2 * d:].T.astype(cdt),
                               DA_HEAD_DIM ** -0.5 * LOG2E)
            oa, ob = [_da_attn(scalars, q, k, vt, da_subln[j][:, None], batch, seq, row0, 1.0 - lambda_init)
                      for batch, seq, row0 in groups]
            x = _out_proj(x, oa, ob, da_wo[j].astype(cdt))
        else:
            qr, kvr = MLA_Q_RANK, MLA_KV_RANK
            wd = mla_w_down[j]
            w_rope = wd[:, qr + kvr:]
            wd = jnp.concatenate([wd[:, :qr + kvr], jnp.pad(w_rope, ((0, 0), (0, zpad))),
                                  jnp.pad(_pair_swap(w_rope), ((0, 0), (0, zpad)))], axis=1)
            wuq = mla_w_uq[j].reshape(qr, MLA_HEADS, MLA_NOPE + MLA_ROPE)
            wq_rope = wuq[:, :, MLA_NOPE:]
            pad3 = ((0, 0), (0, 0), (0, zpad))
            wuq = jnp.concatenate([wuq[:, :, :MLA_NOPE].reshape(qr, hd), jnp.pad(wq_rope, pad3).reshape(qr, hd),
                                   jnp.pad(_pair_swap(wq_rope.reshape(qr, -1)).reshape(wq_rope.shape), pad3).reshape(qr, hd)],
                                  axis=1)
            wukv = mla_w_ukv[j].reshape(kvr, MLA_HEADS, MLA_NOPE + MLA_V)
            wkn = wukv[:, :, :MLA_NOPE].reshape(kvr, hd)
            wvt = wukv[:, :, MLA_NOPE:].reshape(kvr, MLA_HEADS * MLA_V).T
            q, kn, kr, vt = _mla_proj(x, g_mix, wd.astype(cdt), mla_q_norm[j][None, :], mla_kv_norm[j][None, :],
                                      wuq.astype(cdt), wkn.astype(cdt), wvt.astype(cdt), cos, sin,
                                      (MLA_NOPE + MLA_ROPE) ** -0.5 * LOG2E)
            oa, ob = [_mla_attn(q, kn, kr, vt, batch, seq, row0) for batch, seq, row0 in groups]
            x = _out_proj(x, oa, ob, mla_wo[j].astype(cdt))
        w_router = jnp.pad(jnp.concatenate([moe_w_group[i], moe_w_expert[i]], axis=1),
                           ((0, 0), (0, LANES - N_GROUPS - N_EXPERTS)))
        x = _moe(x, ln_ffn[i][None, :], w_router, tri, moe_w1, moe_w3, moe_w2, i, ln_final[None, :],
                 final=(i == depth - 1))
    return x[:n_p].reshape(bp, sp, d), x[n_p:].reshape(bs, ss, d)
```

```python
import functools
import math

import numpy as np
import jax
import jax.numpy as jnp
from jax import lax
from jax.experimental import pallas as pl
from jax.experimental.pallas import tpu as pltpu

F32 = jnp.float32
MXU_DTYPE = jnp.bfloat16
LOG2E = 1.4426950408889634

NORM_EPS = 1e-6
DA_SUBLN_EPS = 1e-5
ROPE_THETA = 10000.0
DA_HEAD_DIM = 64
MLA_HEADS = 8
MLA_NOPE = 128
MLA_ROPE = 64
MLA_V = 128
MLA_Q_RANK = 384
MLA_KV_RANK = 256
N_GROUPS = 8
EXPERTS_PER_GROUP = 8
N_EXPERTS = N_GROUPS * EXPERTS_PER_GROUP

LANES = 128
TOKEN_TILE = 512
KV_CHUNK = 512
DA_Q_TILE = 512
DA_COLS = 256
MLA_Q_TILE = 1024
MLA_COLS = 256
EXPERT_ROWS = 256
VMEM_LIMIT = 56 * 1024 * 1024


def _rms(x, g, eps):
    return x * lax.rsqrt(jnp.mean(x * x, axis=-1, keepdims=True) + eps) * g


def _nt_dot(a, b):
    return lax.dot_general(a, b, (((1,), (1,)), ((), ())), preferred_element_type=F32)


def _params(n_axes):
    return pltpu.CompilerParams(dimension_semantics=("arbitrary",) * n_axes, vmem_limit_bytes=VMEM_LIMIT)


def _da_qkv_kernel(x_ref, g_ref, wq_ref, wk_ref, wvt_ref, q_ref, k_ref, vt_ref, *, q_scale):
    xn = _rms(x_ref[...], g_ref[...], NORM_EPS).astype(MXU_DTYPE)
    q_ref[...] = (jnp.dot(xn, wq_ref[...], preferred_element_type=F32) * q_scale).astype(q_ref.dtype)
    k_ref[...] = jnp.dot(xn, wk_ref[...], preferred_element_type=F32).astype(k_ref.dtype)
    vt_ref[0] = _nt_dot(wvt_ref[...], xn).astype(vt_ref.dtype)


def _da_qkv(x, g, wq, wk, wvt, q_scale):
    n, d = x.shape
    tm = TOKEN_TILE
    full = lambda a: pl.BlockSpec(a.shape, lambda i: (0,) * a.ndim)
    return pl.pallas_call(
        functools.partial(_da_qkv_kernel, q_scale=q_scale),
        grid=(n // tm,),
        in_specs=[pl.BlockSpec((tm, d), lambda i: (i, 0)), full(g), full(wq), full(wk), full(wvt)],
        out_specs=[pl.BlockSpec((tm, d), lambda i: (i, 0)), pl.BlockSpec((tm, d), lambda i: (i, 0)),
                   pl.BlockSpec((1, d, tm), lambda i: (i, 0, 0))],
        out_shape=[jax.ShapeDtypeStruct((n, d), MXU_DTYPE), jax.ShapeDtypeStruct((n, d), MXU_DTYPE),
                   jax.ShapeDtypeStruct((n // tm, d, tm), MXU_DTYPE)],
        compiler_params=_params(1), name="da_qkv",
    )(x, g, wq, wk, wvt)


def _mla_proj_kernel(x_ref, g_ref, wd_ref, qg_ref, kvg_ref, wuq_ref, wkn_ref, wvt_ref, cos_ref, sin_ref,
                     q_ref, kn_ref, kr_ref, vt_ref, *, q_scale):
    xn = _rms(x_ref[...], g_ref[...], NORM_EPS).astype(MXU_DTYPE)
    down = jnp.dot(xn, wd_ref[...], preferred_element_type=F32)
    qr, kvr = MLA_Q_RANK, MLA_KV_RANK
    c_q = _rms(down[:, :qr], qg_ref[...], NORM_EPS).astype(MXU_DTYPE)
    c_kv = _rms(down[:, qr:qr + kvr], kvg_ref[...], NORM_EPS).astype(MXU_DTYPE)
    cos, sin = cos_ref[...], sin_ref[...]
    kr = down[:, qr + kvr:qr + kvr + LANES] * cos + down[:, qr + kvr + LANES:] * sin
    kr_ref[...] = kr.astype(kr_ref.dtype)
    hd = MLA_HEADS * LANES
    q = jnp.dot(c_q, wuq_ref[...], preferred_element_type=F32)
    q_ref[:, :hd] = (q[:, :hd] * q_scale).astype(q_ref.dtype)
    for h in range(MLA_HEADS):
        a = q[:, hd + h * LANES:hd + (h + 1) * LANES]
        b = q[:, 2 * hd + h * LANES:2 * hd + (h + 1) * LANES]
        q_ref[:, hd + h * LANES:hd + (h + 1) * LANES] = ((a * cos + b * sin) * q_scale).astype(q_ref.dtype)
    kn_ref[...] = jnp.dot(c_kv, wkn_ref[...], preferred_element_type=F32).astype(kn_ref.dtype)
    vt_ref[0] = _nt_dot(wvt_ref[...], c_kv).astype(vt_ref.dtype)


def _mla_proj(x, g, wd, qg, kvg, wuq, wkn, wvt, cos, sin, q_scale):
    n, d = x.shape
    tm = TOKEN_TILE
    hd = MLA_HEADS * LANES
    full = lambda a: pl.BlockSpec(a.shape, lambda i: (0,) * a.ndim)
    row = lambda w: pl.BlockSpec((tm, w), lambda i: (i, 0))
    return pl.pallas_call(
        functools.partial(_mla_proj_kernel, q_scale=q_scale),
        grid=(n // tm,),
        in_specs=[row(d), full(g), full(wd), full(qg), full(kvg), full(wuq), full(wkn), full(wvt),
                  row(LANES), row(LANES)],
        out_specs=[row(2 * hd), row(hd), row(LANES), pl.BlockSpec((1, hd, tm), lambda i: (i, 0, 0))],
        out_shape=[jax.ShapeDtypeStruct((n, 2 * hd), MXU_DTYPE), jax.ShapeDtypeStruct((n, hd), MXU_DTYPE),
                   jax.ShapeDtypeStruct((n, LANES), MXU_DTYPE),
                   jax.ShapeDtypeStruct((n // tm, hd, tm), MXU_DTYPE)],
        compiler_params=_params(1), name="mla_proj",
    )(x, g, wd, qg, kvg, wuq, wkn, wvt, cos, sin)


def _out_proj_kernel(x_ref, oa_ref, ob_ref, w_ref, y_ref, *, a_tiles):
    @pl.when(pl.program_id(0) < a_tiles)
    def _():
        y_ref[...] = x_ref[...] + jnp.dot(oa_ref[...], w_ref[...], preferred_element_type=F32)

    @pl.when(pl.program_id(0) >= a_tiles)
    def _():
        y_ref[...] = x_ref[...] + jnp.dot(ob_ref[...], w_ref[...], preferred_element_type=F32)


def _out_proj(x, oa, ob, w):
    n, d = x.shape
    tm = TOKEN_TILE
    a_tiles = oa.shape[0] // tm
    assert oa.shape[0] % tm == 0 and ob.shape[0] % tm == 0 and oa.shape[0] + ob.shape[0] == n
    return pl.pallas_call(
        functools.partial(_out_proj_kernel, a_tiles=a_tiles),
        grid=(n // tm,),
        in_specs=[pl.BlockSpec((tm, d), lambda i: (i, 0)),
                  pl.BlockSpec((tm, oa.shape[1]), lambda i: (jnp.minimum(i, a_tiles - 1), 0)),
                  pl.BlockSpec((tm, ob.shape[1]), lambda i: (jnp.maximum(i - a_tiles, 0), 0)),
                  pl.BlockSpec(w.shape, lambda i: (0, 0))],
        out_specs=pl.BlockSpec((tm, d), lambda i: (i, 0)),
        out_shape=jax.ShapeDtypeStruct((n, d), F32),
        compiler_params=_params(1), name="out_proj",
    )(x, oa, ob, w)


FLASH_STRIP = 32
FRAME_HEADROOM = 60.0
FLASH_ROUNDS = 2
FLASH_SLOTS = 3


def _flash_scratch(tk, w, dv):
    n = FLASH_SLOTS
    return ([pltpu.VMEM((tk, w), F32)] * n + [pltpu.VMEM((tk, w), MXU_DTYPE)] * n + [pltpu.VMEM((1, w), F32)] * (2 * n)
            + [pltpu.VMEM((1, w), F32), pltpu.VMEM((1, w), F32), pltpu.VMEM((dv, w), F32)])


class _Frame:
    def __init__(self, order, operands, weighted):
        self.order, self.operands, self.weighted = order, operands, weighted


def _flash_t(q, k_ref, vt_ref, n_chunks, tk, scratch, bias_fn, cols, frame):
    n = FLASH_SLOTS
    s_bufs, p_bufs, c_bufs, a_bufs = (scratch[i * n:(i + 1) * n] for i in range(4))
    m_ref, l_ref, acc_ref = scratch[4 * n:4 * n + 3]
    assert n_chunks >= 2
    w = q.shape[0]
    groups = [slice(g, g + cols) for g in range(0, w, cols)]

    def scores_g(c, slot, g):
        k = k_ref[pl.ds(pl.multiple_of(c * tk, tk), tk), :]
        st = _nt_dot(k, q[g, :])
        if bias_fn is not None:
            st = bias_fn(st, c, g)
        cmax = None
        for r in range(0, tk, FLASH_STRIP):
            strip = st[r:r + FLASH_STRIP, :]
            s_bufs[slot][r:r + FLASH_STRIP, g] = strip
            cmax = strip if cmax is None else jnp.maximum(cmax, strip)
        c_bufs[slot][:, g] = jnp.max(cmax, axis=0, keepdims=True)

    def softmax_g(slot, g):
        m_prev = m_ref[:, g]
        m_new = jnp.maximum(m_prev, c_bufs[slot][:, g])
        alpha = jnp.exp2(m_prev - m_new)
        m_ref[:, g] = m_new
        a_bufs[slot][:, g] = alpha
        psum = None
        for r in range(0, tk, FLASH_STRIP):
            p = jnp.exp2(s_bufs[slot][r:r + FLASH_STRIP, g] - m_new)
            psum = p if psum is None else psum + p
            p_bufs[slot][r:r + FLASH_STRIP, g] = p.astype(MXU_DTYPE)
        l_ref[:, g] = alpha * l_ref[:, g] + jnp.sum(psum, axis=0, keepdims=True)

    def values_g(c, slot, g):
        tv = vt_ref.shape[2]
        pv = None
        for j in range(tk // tv):
            part = jnp.dot(vt_ref[c * (tk // tv) + j], p_bufs[slot][j * tv:(j + 1) * tv, g],
                           preferred_element_type=F32)
            pv = part if pv is None else pv + part
        acc_ref[:, g] = acc_ref[:, g] * a_bufs[slot][:, g] + pv

    def scores(c, slot):
        for g in groups:
            scores_g(c, slot, g)

    def softmax(slot):
        for g in groups:
            softmax_g(slot, g)

    def values(c, slot):
        for g in groups:
            values_g(c, slot, g)

    def step(c, slot):
        for g in groups:
            values_g(c - 1, (slot - 1) % n, g)
            scores_g(c + 1, (slot + 1) % n, g)
            softmax_g(slot, g)

    per_trip = n * FLASH_ROUNDS

    def rescaling_pass():
        m_ref[...] = jnp.full(m_ref.shape, -jnp.inf, F32)
        l_ref[...] = jnp.zeros(l_ref.shape, F32)
        acc_ref[...] = jnp.zeros(acc_ref.shape, F32)
        scores(0, 0)
        scores(1, 1)
        softmax(0)
        trips = (n_chunks - 2) // per_trip

        def body(j, carry):
            for r in range(per_trip):
                step(1 + per_trip * j + r, (1 + r) % n)
            return carry

        lax.fori_loop(0, trips, body, 0)
        for c in range(1 + per_trip * trips, n_chunks - 1):
            step(c, c % n)
        values(n_chunks - 2, (n_chunks - 2) % n)
        softmax((n_chunks - 1) % n)
        values(n_chunks - 1, (n_chunks - 1) % n)

    if frame is None:
        rescaling_pass()
        return

    excess_ref = c_bufs[0]

    def frame_values_g(c, slot, g):
        tv = vt_ref.shape[2]
        pv = None
        for j in range(tk // tv):
            part = jnp.dot(vt_ref[c * (tk // tv) + j], p_bufs[slot][j * tv:(j + 1) * tv, g],
                           preferred_element_type=F32)
            pv = part if pv is None else pv + part
        if frame.weighted:
            pv = pv * a_bufs[slot][:, g]
        acc_ref[:, g] = acc_ref[:, g] + pv

    def frame_scores_g(c, slot, g):
        qg, k_extra, shift, weight = frame.operands(c, g)
        k = k_ref[pl.ds(pl.multiple_of(c * tk, tk), tk), :]
        if k_extra is not None:
            k = jnp.concatenate([k, k_extra], axis=1)
        st = _nt_dot(k, qg)
        base = m_ref[:, g] if shift is None else m_ref[:, g] + shift
        cmax = psum = None
        for r in range(0, tk, FLASH_STRIP):
            strip = st[r:r + FLASH_STRIP, :]
            cmax = strip if cmax is None else jnp.maximum(cmax, strip)
            p = jnp.exp2(strip - base)
            psum = p if psum is None else psum + p
            p_bufs[slot][r:r + FLASH_STRIP, g] = p.astype(MXU_DTYPE)
        excess_ref[:, g] = jnp.maximum(excess_ref[:, g], jnp.max(cmax, axis=0, keepdims=True) - base)
        csum = jnp.sum(psum, axis=0, keepdims=True)
        if frame.weighted:
            a_bufs[slot][:, g] = weight
            csum = csum * weight
        l_ref[:, g] = l_ref[:, g] + csum

    def frame_step(j, slot):
        for g in groups:
            frame_values_g(frame.order(j - 1), 1 - slot, g)
            frame_scores_g(frame.order(j), slot, g)

    acc_ref[...] = jnp.zeros(acc_ref.shape, F32)
    excess_ref[...] = jnp.zeros(excess_ref.shape, F32)
    c0 = frame.order(0)
    for g in groups:
        k = k_ref[pl.ds(pl.multiple_of(c0 * tk, tk), tk), :]
        st = _nt_dot(k, q[g, :])
        if bias_fn is not None:
            st = bias_fn(st, c0, g)
        top = jnp.max(st, axis=0, keepdims=True)
        m_ref[:, g] = top
        p = jnp.exp2(st - top)
        l_ref[:, g] = jnp.sum(p, axis=0, keepdims=True)
        p_bufs[0][:, g] = p.astype(MXU_DTYPE)
        if frame.weighted:
            a_bufs[0][:, g] = jnp.ones((1, cols), F32)
    frame_trips = (n_chunks - 1) // per_trip

    def frame_body(t, carry):
        for r in range(per_trip):
            frame_step(1 + per_trip * t + r, (1 + r) % 2)
        return carry

    assert per_trip % 2 == 0
    lax.fori_loop(0, frame_trips, frame_body, 0)
    for j in range(1 + per_trip * frame_trips, n_chunks):
        frame_step(j, j % 2)
    for g in groups:
        frame_values_g(frame.order(n_chunks - 1), (n_chunks - 1) % 2, g)

    @pl.when(jnp.max(excess_ref[...]) > FRAME_HEADROOM)
    def _():
        rescaling_pass()


POS_SPLIT = 128
SLOPE_PIECES = 3


def _row_index_columns(tk, dtype):
    i = np.arange(tk)
    cols = np.zeros((tk, LANES), np.float32)
    for piece in range(SLOPE_PIECES):
        cols[:, 2 * piece] = i // POS_SPLIT
        cols[:, 2 * piece + 1] = i % POS_SPLIT
    return jnp.asarray(cols, dtype)


def _kv_chunk(seq):
    return min(KV_CHUNK, max(TOKEN_TILE, seq // 2))


def _da_attn_kernel(sc_ref, q_ref, k_ref, vt_ref, g_ref, pos_ref, o_ref, dist_ref, qa_ref, *scratch,
                    tq, tk, n_chunks, n_heads, out_scale):
    l_ref, acc_ref = scratch[4 * FLASH_SLOTS + 1:]
    h, qi = pl.program_id(1), pl.program_id(2)
    slope, lam = sc_ref[h], sc_ref[n_heads]
    q = q_ref[...]
    lane = lax.broadcasted_iota(jnp.int32, q.shape, 1)
    zero = jnp.zeros_like(q)
    d = q.shape[1] // 2
    qs = jnp.concatenate([jnp.where(lane < d, q, zero), jnp.where(lane >= d, q, zero)], axis=0)
    row = lax.broadcasted_iota(jnp.int32, (tk, tq), 0)
    col = lax.broadcasted_iota(jnp.int32, (tk, tq), 1)
    dist_ref[...] = (row - col - qi * tq).astype(F32)

    assert tq % DA_COLS == 0

    def bias_fn(st, c, g):
        off = g.start % tq
        dist = dist_ref[:, off:off + DA_COLS] + lax.convert_element_type(c * tk, F32)
        return st - slope * jnp.abs(dist)

    pieces = [sc_ref[n_heads + 1 + SLOPE_PIECES * h + i] for i in range(SLOPE_PIECES)]
    coef = jnp.zeros(qs.shape, F32)
    lane2 = lax.broadcasted_iota(jnp.int32, qs.shape, 1)
    for i, piece in enumerate(pieces):
        coef = jnp.where(lane2 == 2 * i, piece * float(POS_SPLIT), jnp.where(lane2 == 2 * i + 1, piece, coef))
    qa_ref[0] = jnp.concatenate([qs, (-coef).astype(qs.dtype)], axis=1)
    qa_ref[1] = jnp.concatenate([qs, coef.astype(qs.dtype)], axis=1)
    home = (qi * tq) // tk
    qpos = (qi * tq + lax.broadcasted_iota(jnp.int32, (1, tq), 1)).astype(F32)

    def operands(c, g):
        before = c < home
        start = lax.convert_element_type(c * tk, F32)
        shift = jnp.where(before, slope * (tk - 1), 0.0)
        edge = jnp.where(before, (start + (tk - 1)) - qpos, qpos - start)
        weight = jnp.exp2(slope * edge)
        weight = weight[:, g.start % tq:g.start % tq + DA_COLS]
        return qa_ref[before.astype(jnp.int32)][g, :], pos_ref[...], shift, weight

    frame = _Frame(order=lambda j: lax.rem(home + j, jnp.int32(n_chunks)), operands=operands, weighted=True)
    _flash_t(qs, k_ref, vt_ref, n_chunks, tk, scratch, bias_fn, DA_COLS, frame)
    o = acc_ref[...] * (1.0 / l_ref[...])
    o = o[:, :tq] - lam * o[:, tq:]
    o = o * lax.rsqrt(jnp.mean(o * o, axis=0, keepdims=True) + DA_SUBLN_EPS) * g_ref[...] * out_scale
    o_ref[...] = o.T.astype(o_ref.dtype)


def _da_attn(scalars, q, k, vt, subln_col, batch, seq, row0, out_scale):
    dm = q.shape[1]
    tq, tk, tv = DA_Q_TILE, _kv_chunk(seq), vt.shape[2]
    hw = 2 * DA_HEAD_DIM
    n_heads = dm // hw
    assert row0 % seq == 0 and seq % tq == 0 and seq % tk == 0 and tk % tv == 0
    s0, q0, nq, n_chunks = row0 // seq, row0 // tq, seq // tq, seq // tk
    assert tk % tq == 0 and tk % POS_SPLIT == 0 and tk // POS_SPLIT <= 256
    pos = _row_index_columns(tk, q.dtype)
    return pl.pallas_call(
        functools.partial(_da_attn_kernel, tq=tq, tk=tk, n_chunks=n_chunks, n_heads=n_heads, out_scale=out_scale),
        grid_spec=pltpu.PrefetchScalarGridSpec(
            num_scalar_prefetch=1, grid=(batch, n_heads, nq),
            in_specs=[pl.BlockSpec((tq, hw), lambda b, h, i, sc: (q0 + b * nq + i, h)),
                      pl.BlockSpec((seq, hw), lambda b, h, i, sc: (s0 + b, h)),
                      pl.BlockSpec((seq // tv, hw, tv), lambda b, h, i, sc: (s0 + b, h, 0)),
                      pl.BlockSpec(subln_col.shape, lambda b, h, i, sc: (0, 0)),
                      pl.BlockSpec(pos.shape, lambda b, h, i, sc: (0, 0))],
            out_specs=pl.BlockSpec((tq, hw), lambda b, h, i, sc: (b * nq + i, h)),
            scratch_shapes=[pltpu.VMEM((tk, tq), F32), pltpu.VMEM((2, 2 * tq, hw + LANES), q.dtype)]
            + _flash_scratch(tk, 2 * tq, hw)),
        out_shape=jax.ShapeDtypeStruct((batch * seq, dm), MXU_DTYPE),
        compiler_params=_params(3), name="da_attn",
    )(scalars, q, k, vt, subln_col, pos)


def _mla_attn_kernel(qn_ref, qr_ref, kn_ref, kr_ref, vt_ref, o_ref, kfull_ref, *scratch, tk, n_chunks):
    l_ref, acc_ref = scratch[4 * FLASH_SLOTS + 1:]

    @pl.when(pl.program_id(2) == 0)
    def _():
        def copy(c, carry):
            rows = pl.ds(pl.multiple_of(c * tk, tk), tk)
            kfull_ref[rows, :LANES] = kn_ref[rows, :]
            kfull_ref[rows, LANES:] = kr_ref[rows, :]
            return carry
        lax.fori_loop(0, n_chunks, copy, 0)

    q = jnp.concatenate([qn_ref[...], qr_ref[...]], axis=1)
    frame = _Frame(order=lambda j: j, operands=lambda c, g: (q[g, :], None, None, None), weighted=False)
    _flash_t(q, kfull_ref, vt_ref, n_chunks, tk, scratch, None, MLA_COLS, frame)
    o = acc_ref[...] * (1.0 / l_ref[...])
    o_ref[...] = o.T.astype(o_ref.dtype)


def _mla_attn(q, kn, kr, vt, batch, seq, row0):
    n_heads = MLA_HEADS
    tq, tk, tv = MLA_Q_TILE, _kv_chunk(seq), vt.shape[2]
    assert row0 % seq == 0 and seq % tq == 0 and seq % tk == 0 and tk % tv == 0
    s0, q0, nq, n_chunks = row0 // seq, row0 // tq, seq // tq, seq // tk
    return pl.pallas_call(
        functools.partial(_mla_attn_kernel, tk=tk, n_chunks=n_chunks),
        grid=(batch, n_heads, nq),
        in_specs=[pl.BlockSpec((tq, LANES), lambda b, h, i: (q0 + b * nq + i, h)),
                  pl.BlockSpec((tq, LANES), lambda b, h, i: (q0 + b * nq + i, n_heads + h)),
                  pl.BlockSpec((seq, LANES), lambda b, h, i: (s0 + b, h)),
                  pl.BlockSpec((seq, LANES), lambda b, h, i: (s0 + b, 0)),
                  pl.BlockSpec((seq // tv, MLA_V, tv), lambda b, h, i: (s0 + b, h, 0))],
        out_specs=pl.BlockSpec((tq, MLA_V), lambda b, h, i: (b * nq + i, h)),
        out_shape=jax.ShapeDtypeStruct((batch * seq, n_heads * MLA_V), MXU_DTYPE),
        scratch_shapes=[pltpu.VMEM((seq, 2 * LANES), MXU_DTYPE)] + _flash_scratch(tk, tq, MLA_V),
        compiler_params=_params(3), name="mla_attn",
    )(q, q, kn, kr, vt)


R_E0, R_E1, R_RANK0, R_RANK1, R_GATE0, R_GATE1 = range(6)


def _router_kernel(x_ref, g_ref, w_ref, tri_ref, rec_ref, cnt_ref, base_ref):
    i = pl.program_id(0)

    @pl.when(i == 0)
    def _():
        base_ref[...] = jnp.zeros(base_ref.shape, F32)

    xn = _rms(x_ref[...], g_ref[...], NORM_EPS)
    logits = jnp.dot(xn, w_ref[...], preferred_element_type=F32, precision=lax.Precision.HIGHEST)
    tm = logits.shape[0]
    lane = lax.broadcasted_iota(jnp.int32, logits.shape, 1)
    neg = jnp.float32(-jnp.inf)

    def first_lane(mask):
        return jnp.min(jnp.where(mask, lane, LANES), axis=1, keepdims=True)

    is_g = lane < N_GROUPS
    lg = jnp.where(is_g, logits, neg)
    mg = jnp.max(lg, axis=1, keepdims=True)
    pg_sel = 1.0 / jnp.sum(jnp.exp(lg - mg), axis=1, keepdims=True)
    g_sel = first_lane(lg == mg)
    e_lo = N_GROUPS + EXPERTS_PER_GROUP * g_sel
    in_e = (lane >= e_lo) & (lane < e_lo + EXPERTS_PER_GROUP)
    le = jnp.where(in_e, logits, neg)
    me = jnp.max(le, axis=1, keepdims=True)
    ee = jnp.exp(le - me)
    pe = ee / jnp.sum(ee, axis=1, keepdims=True)
    pe = jnp.where(in_e, pe, -1.0)
    p0 = jnp.max(pe, axis=1, keepdims=True)
    i0 = first_lane(pe == p0)
    pe1 = jnp.where(lane == i0, -1.0, pe)
    p1 = jnp.max(pe1, axis=1, keepdims=True)
    i1 = first_lane(pe1 == p1)
    gate0 = pg_sel * p0 / (p0 + p1)
    gate1 = pg_sel * p1 / (p0 + p1)
    hit0, hit1 = lane == i0, lane == i1
    oh = jnp.concatenate([jnp.where(hit0, 1.0, 0.0), jnp.where(hit1, 1.0, 0.0)], axis=1)
    before = jnp.dot(tri_ref[...], oh.astype(MXU_DTYPE), preferred_element_type=F32)
    base = base_ref[...]
    tot0 = jnp.sum(oh[:, :LANES], axis=0, keepdims=True)
    tot1 = jnp.sum(oh[:, LANES:], axis=0, keepdims=True)
    rank0 = jnp.sum(jnp.where(hit0, before[:, :LANES] + base, 0.0), axis=1, keepdims=True)
    rank1 = jnp.sum(jnp.where(hit1, before[:, LANES:] + (base + tot0), 0.0), axis=1, keepdims=True)
    base = base + tot0 + tot1
    base_ref[...] = base
    cnt_ref[...] = base
    rec = jnp.zeros(logits.shape, F32)
    for pos, val in ((R_E0, (i0 - N_GROUPS).astype(F32)), (R_E1, (i1 - N_GROUPS).astype(F32)),
                     (R_RANK0, rank0), (R_RANK1, rank1), (R_GATE0, gate0), (R_GATE1, gate1)):
        rec = jnp.where(lane == pos, val, rec)
    rec_ref[...] = rec


def _router(x, g, w_router, tri):
    n, d = x.shape
    tm = TOKEN_TILE
    full = lambda a: pl.BlockSpec(a.shape, lambda i: (0,) * a.ndim)
    return pl.pallas_call(
        _router_kernel,
        grid=(n // tm,),
        in_specs=[pl.BlockSpec((tm, d), lambda i: (i, 0)), full(g), full(w_router), full(tri)],
        out_specs=[pl.BlockSpec((tm, LANES), lambda i: (i, 0)), pl.BlockSpec((1, LANES), lambda i: (0, 0))],
        out_shape=[jax.ShapeDtypeStruct((n, LANES), F32), jax.ShapeDtypeStruct((1, LANES), F32)],
        scratch_shapes=[pltpu.VMEM((1, LANES), F32)],
        compiler_params=_params(1), name="moe_router",
    )(x, g, w_router, tri)


ROW_TILE = 8


def _row_copy(src_ref, src_row, dst_ref, dst_row, sem):
    src = src_ref.at[pl.ds(pl.multiple_of(src_row * ROW_TILE, ROW_TILE), ROW_TILE), :]
    dst = dst_ref.at[pl.ds(pl.multiple_of(dst_row * ROW_TILE, ROW_TILE), ROW_TILE), :]
    return pltpu.make_async_copy(src, dst, sem)


def _to_row_tiles(ref, x):
    rows = x.shape[0]
    for c in range(ROW_TILE):
        ref[pl.ds(c, rows, stride=ROW_TILE), :] = x[:, c * LANES:(c + 1) * LANES]


def _from_row_tiles(ref):
    rows = ref.shape[0] // ROW_TILE
    return jnp.concatenate([ref[pl.ds(c, rows, stride=ROW_TILE), :] for c in range(ROW_TILE)], axis=1)


def _dispatch_kernel(dest_ref, x_ref, g_ref, xs_in_ref, xs_ref, xn_ref, sem, *, tm):
    del xs_in_ref
    _to_row_tiles(xn_ref, _rms(x_ref[...], g_ref[...], NORM_EPS))

    def issue(r, carry):
        for c in range(2):
            _row_copy(xn_ref, r, xs_ref, dest_ref[2 * r + c], sem).start(priority=c)
        return carry
    lax.fori_loop(0, tm, issue, 0, unroll=8)

    def drain(r, carry):
        for c in range(2):
            _row_copy(xn_ref, 0, xs_ref, 0, sem).wait()
        return carry
    lax.fori_loop(0, tm, drain, 0, unroll=8)


def _dispatch(dest_flat, x, g, xs_zero):
    n, d = x.shape
    tm = TOKEN_TILE
    return pl.pallas_call(
        functools.partial(_dispatch_kernel, tm=tm),
        grid=(n // tm,),
        in_specs=[pl.BlockSpec((2 * tm,), lambda i: (i,), memory_space=pltpu.SMEM),
                  pl.BlockSpec((tm, d), lambda i: (i, 0)), pl.BlockSpec(g.shape, lambda i: (0, 0)),
                  pl.BlockSpec(memory_space=pl.ANY)],
        out_specs=pl.BlockSpec(memory_space=pl.ANY),
        out_shape=jax.ShapeDtypeStruct(xs_zero.shape, F32),
        scratch_shapes=[pltpu.VMEM((tm * ROW_TILE, LANES), F32), pltpu.SemaphoreType.DMA(())],
        input_output_aliases={3: 0},
        compiler_params=_params(1), name="moe_dispatch",
    )(dest_flat, x, g, xs_zero)


def _expert_kernel(be_ref, nv_ref, xs_ref, w1_ref, w3_ref, w2_ref, y_ref, w1c_ref, w3c_ref, w2c_ref):
    b = pl.program_id(0)

    @pl.when((b == 0) | (be_ref[b] != be_ref[jnp.maximum(b - 1, 0)]))
    def _():
        w1c_ref[...] = w1_ref[0, 0].astype(MXU_DTYPE)
        w3c_ref[...] = w3_ref[0, 0].astype(MXU_DTYPE)
        w2c_ref[...] = w2_ref[0, 0].astype(MXU_DTYPE)

    @pl.when(b < nv_ref[0])
    def _():
        xb = _from_row_tiles(xs_ref).astype(MXU_DTYPE)
        h1 = jnp.dot(xb, w1c_ref[...], preferred_element_type=F32)
        h3 = jnp.dot(xb, w3c_ref[...], preferred_element_type=F32)
        h = (h1 * jax.nn.sigmoid(h1) * h3).astype(MXU_DTYPE)
        _to_row_tiles(y_ref, jnp.dot(h, w2c_ref[...], preferred_element_type=F32))

    @pl.when(b >= nv_ref[0])
    def _():
        y_ref[...] = jnp.zeros(y_ref.shape, F32)


def _experts(block_expert, n_valid, xs, w1, w3, w2, layer):
    rb = EXPERT_ROWS
    d, de = w1.shape[2:]
    assert d == ROW_TILE * LANES
    return pl.pallas_call(
        _expert_kernel,
        grid_spec=pltpu.PrefetchScalarGridSpec(
            num_scalar_prefetch=2, grid=(xs.shape[0] // (rb * ROW_TILE),),
            in_specs=[pl.BlockSpec((rb * ROW_TILE, LANES), lambda b, be, nv: (b, 0)),
                      pl.BlockSpec((1, 1, d, de), lambda b, be, nv: (layer, be[b], 0, 0)),
                      pl.BlockSpec((1, 1, d, de), lambda b, be, nv: (layer, be[b], 0, 0)),
                      pl.BlockSpec((1, 1, de, d), lambda b, be, nv: (layer, be[b], 0, 0))],
            out_specs=pl.BlockSpec((rb * ROW_TILE, LANES), lambda b, be, nv: (b, 0)),
            scratch_shapes=[pltpu.VMEM((d, de), MXU_DTYPE), pltpu.VMEM((d, de), MXU_DTYPE),
                            pltpu.VMEM((de, d), MXU_DTYPE)]),
        out_shape=jax.ShapeDtypeStruct(xs.shape, F32),
        compiler_params=_params(1), name="moe_experts",
    )(block_expert, n_valid, xs, w1, w3, w2)


def _combine_kernel(dest_ref, x_ref, rec_ref, gfin_ref, yb_ref, o_ref, ybuf_ref, sem, *, tm, final):
    def issue(r, carry):
        for c in range(2):
            _row_copy(yb_ref, dest_ref[2 * r + c], ybuf_ref.at[c], r, sem).start(priority=c)
        return carry
    lax.fori_loop(0, tm, issue, 0, unroll=8)

    def drain(r, carry):
        for c in range(2):
            _row_copy(yb_ref, 0, ybuf_ref.at[c], 0, sem).wait()
        return carry
    lax.fori_loop(0, tm, drain, 0, unroll=8)

    rec = rec_ref[...]
    out = (x_ref[...] + rec[:, R_GATE0:R_GATE0 + 1] * _from_row_tiles(ybuf_ref.at[0])
           + rec[:, R_GATE1:R_GATE1 + 1] * _from_row_tiles(ybuf_ref.at[1]))
    if final:
        out = _rms(out, gfin_ref[...], NORM_EPS)
    o_ref[...] = out


def _combine(dest_flat, x, rec, g_final, yb, final):
    n, d = x.shape
    tm = TOKEN_TILE
    return pl.pallas_call(
        functools.partial(_combine_kernel, tm=tm, final=final),
        grid=(n // tm,),
        in_specs=[pl.BlockSpec((2 * tm,), lambda i: (i,), memory_space=pltpu.SMEM),
                  pl.BlockSpec((tm, d), lambda i: (i, 0)), pl.BlockSpec((tm, LANES), lambda i: (i, 0)),
                  pl.BlockSpec(g_final.shape, lambda i: (0, 0)), pl.BlockSpec(memory_space=pl.ANY)],
        out_specs=pl.BlockSpec((tm, d), lambda i: (i, 0)),
        out_shape=jax.ShapeDtypeStruct((n, d), F32),
        scratch_shapes=[pltpu.VMEM((2, tm * ROW_TILE, LANES), F32), pltpu.SemaphoreType.DMA(())],
        compiler_params=_params(1), name="moe_combine",
    )(dest_flat, x, rec, g_final, yb)


def _moe(x, g, w_router, tri, w1, w3, w2, layer, g_final, final):
    n, d = x.shape
    rb = EXPERT_ROWS
    rec, counts = _router(x, g, w_router, tri)
    counts = counts[0, N_GROUPS:N_GROUPS + N_EXPERTS].astype(jnp.int32)
    padded = (counts + rb - 1) // rb * rb
    pad_end = jnp.cumsum(padded)
    pad_start = pad_end - padded
    n_blocks = (2 * n + N_EXPERTS * (rb - 1)) // rb
    expert = rec[:, R_E0:R_E1 + 1].astype(jnp.int32)
    rank = rec[:, R_RANK0:R_RANK1 + 1].astype(jnp.int32)
    before = expert[..., None] > jnp.arange(N_EXPERTS, dtype=jnp.int32)
    dest = (jnp.sum(jnp.where(before, padded, 0), axis=-1) + rank).reshape(-1)
    block_start = jnp.arange(n_blocks, dtype=jnp.int32) * rb
    block_expert = jnp.minimum(jnp.searchsorted(pad_end, block_start, side="right"), N_EXPERTS - 1).astype(jnp.int32)
    n_valid = (pad_end[-1:] // rb).astype(jnp.int32)
    xs = _dispatch(dest, x, g, jnp.zeros((n_blocks * rb * ROW_TILE, LANES), F32))
    yb = _experts(block_expert, n_valid, xs, w1, w3, w2, layer)
    return _combine(dest, x, rec, g_final, yb, final)


def _rope_tables(positions):
    inv = 1.0 / (ROPE_THETA ** (jnp.arange(0, MLA_ROPE, 2, dtype=F32) / MLA_ROPE))
    ang = positions.astype(F32)[:, None] * inv[None, :]
    cos = jnp.repeat(jnp.cos(ang), 2, axis=1)
    sin = jnp.stack([-jnp.sin(ang), jnp.sin(ang)], axis=-1).reshape(ang.shape[0], MLA_ROPE)
    pad = ((0, 0), (0, LANES - MLA_ROPE))
    return jnp.pad(cos, pad), jnp.pad(sin, pad)


def _pair_swap(w):
    return w.reshape(w.shape[0], -1, 2)[:, :, ::-1].reshape(w.shape)


def kernel(x_prompt, x_sample, ln_mix, ln_ffn, ln_final, da_wqkv, da_wo, da_lambda_q1, da_lambda_k1, da_lambda_q2, da_lambda_k2, da_subln, mla_w_down, mla_q_norm, mla_w_uq, mla_kv_norm, mla_w_ukv, mla_wo, moe_w_group, moe_w_expert, moe_w1, moe_w3, moe_w2):
    bp, sp, d = x_prompt.shape
    bs, ss, _ = x_sample.shape
    n_p, n_s = bp * sp, bs * ss
    groups = ((bp, sp, 0), (bs, ss, n_p))
    x = jnp.concatenate([x_prompt.reshape(n_p, d), x_sample.reshape(n_s, d)], axis=0)
    depth = ln_mix.shape[0]
    cdt = MXU_DTYPE

    positions = jnp.concatenate([jnp.tile(jnp.arange(sp), bp), jnp.tile(jnp.arange(ss), bs)])
    cos, sin = _rope_tables(positions)
    tri = jnp.tril(jnp.ones((TOKEN_TILE, TOKEN_TILE), F32), -1).astype(cdt)
    n_da_heads = d // (2 * DA_HEAD_DIM)
    slopes = 2.0 ** (-8.0 * np.arange(1, n_da_heads + 1) / n_da_heads) * LOG2E
    hd = MLA_HEADS * LANES
    zpad = LANES - MLA_ROPE
    rest, parts = jnp.asarray(slopes, F32), []
    for _ in range(SLOPE_PIECES):
        parts.append(rest.astype(jnp.bfloat16).astype(F32))
        rest = rest - parts[-1]
    slope_pieces = jnp.stack(parts, axis=1).reshape(-1)

    for i in range(depth):
        j = i // 2
        g_mix = ln_mix[i][None, :]
        if i % 2 == 0:
            lambda_init = 0.8 - 0.6 * math.exp(-0.3 * i)
            lam = (jnp.exp(jnp.sum(da_lambda_q1[j] * da_lambda_k1[j])) - jnp.exp(jnp.sum(da_lambda_q2[j] * da_lambda_k2[j]))
                   + lambda_init)
            scalars = jnp.concatenate([jnp.asarray(slopes, F32), lam[None].astype(F32), slope_pieces])
            w = da_wqkv[j]
            q, k, vt = _da_qkv(x, g_mix, w[:, :d].astype(cdt), w[:, d:2 * d].astype(cdt), w[:, 2 * d:].T.astype(cdt),
                               DA_HEAD_DIM ** -0.5 * LOG2E)
            oa, ob = [_da_attn(scalars, q, k, vt, da_subln[j][:, None], batch, seq, row0, 1.0 - lambda_init)
                      for batch, seq, row0 in groups]
            x = _out_proj(x, oa, ob, da_wo[j].astype(cdt))
        else:
            qr, kvr = MLA_Q_RANK, MLA_KV_RANK
            wd = mla_w_down[j]
            w_rope = wd[:, qr + kvr:]
            wd = jnp.concatenate([wd[:, :qr + kvr], jnp.pad(w_rope, ((0, 0), (0, zpad))),
                                  jnp.pad(_pair_swap(w_rope), ((0, 0), (0, zpad)))], axis=1)
            wuq = mla_w_uq[j].reshape(qr, MLA_HEADS, MLA_NOPE + MLA_ROPE)
            wq_rope = wuq[:, :, MLA_NOPE:]
            pad3 = ((0, 0), (0, 0), (0, zpad))
            wuq = jnp.concatenate([wuq[:, :, :MLA_NOPE].reshape(qr, hd), jnp.pad(wq_rope, pad3).reshape(qr, hd),
                                   jnp.pad(_pair_swap(wq_rope.reshape(qr, -1)).reshape(wq_rope.shape), pad3).reshape(qr, hd)],
                                  axis=1)
            wukv = mla_w_ukv[j].reshape(kvr, MLA_HEADS, MLA_NOPE + MLA_V)
            wkn = wukv[:, :, :MLA_NOPE].reshape(kvr, hd)
            wvt = wukv[:, :, MLA_NOPE:].reshape(kvr, MLA_HEADS * MLA_V).T
            q, kn, kr, vt = _mla_proj(x, g_mix, wd.astype(cdt), mla_q_norm[j][None, :], mla_kv_norm[j][None, :],
                                      wuq.astype(cdt), wkn.astype(cdt), wvt.astype(cdt), cos, sin,
                                      (MLA_NOPE + MLA_ROPE) ** -0.5 * LOG2E)
            oa, ob = [_mla_attn(q, kn, kr, vt, batch, seq, row0) for batch, seq, row0 in groups]
            x = _out_proj(x, oa, ob, mla_wo[j].astype(cdt))
        w_router = jnp.pad(jnp.concatenate([moe_w_group[i], moe_w_expert[i]], axis=1),
                           ((0, 0), (0, LANES - N_GROUPS - N_EXPERTS)))
        x = _moe(x, ln_ffn[i][None, :], w_router, tri, moe_w1, moe_w3, moe_w2, i, ln_final[None, :],
                 final=(i == depth - 1))
    return x[:n_p].reshape(bp, sp, d), x[n_p:].reshape(bs, ss, d)
```
